```python
import math
import jax
import jax.numpy as jnp
from jax import lax
import numpy as np

D_MODEL = 1024
BATCH = 1
SEQ = 16384
DEPTH = 2

GRID_W = 64
CTX_LEN = 256
RMS_EPS = 1e-6
ROPE_BASE = 10000.0
Q_BLOCK = 128
NEG_INF = -1e30

MLA_HEADS = 8
MLA_Q_RANK = 256
MLA_KV_RANK = 128
MLA_NOPE = 64
MLA_ROPE = 32
MLA_V = 64
MLA_SCALE = 1.0 / math.sqrt(MLA_NOPE + MLA_ROPE)

GQA_Q_HEADS = 8
GQA_KV_HEADS = 2
GQA_GROUP = GQA_Q_HEADS // GQA_KV_HEADS
GQA_HEAD_DIM = 64
GQA_SCALE = 1.0 / math.sqrt(GQA_HEAD_DIM)
WINDOW = 128
WIN_BLOCK = 128

OFF_CKV = 0
OFF_KROPE = OFF_CKV + MLA_KV_RANK
OFF_GK = OFF_KROPE + MLA_ROPE
OFF_GV = OFF_GK + GQA_KV_HEADS * GQA_HEAD_DIM
KV_COLS = OFF_GV + GQA_KV_HEADS * GQA_HEAD_DIM
OFF_CQ = KV_COLS
OFF_GQ = OFF_CQ + MLA_Q_RANK
PROJ_COLS = OFF_GQ + GQA_Q_HEADS * GQA_HEAD_DIM
MIX_WIDTH = MLA_HEADS * MLA_V + GQA_Q_HEADS * GQA_HEAD_DIM

HY_ORDER = 2
HY_DIRS = 2
HY_BANDS = 16
HY_EMB = 2 * HY_BANDS + 1
HY_FILTER_FF = 64
HY_SHORT = 3
HY_DECAY_TARGET = 1e-2
HY_FAST_DECAY = 0.3
HY_SLOW_DECAY = 1.5

N_EXPERTS = 32
TOP_K = 4
EXPERT_FF = 1024
SWIGLU_LIMIT = 7.0
SWIGLU_ALPHA = 1.702

N_ATTN_LAYERS = (DEPTH + 1) // 2
N_HYENA_LAYERS = DEPTH // 2

kernel_name = 'hybrid_mla_swa_hyena_moe_dit'


def rmsnorm(x, g):
    xf = x.astype(jnp.float32)
    y = xf * lax.rsqrt(jnp.mean(xf * xf, axis=-1, keepdims=True) + RMS_EPS)
    return (y * g.astype(jnp.float32)).astype(x.dtype)


def modulate(h, shift, scale):
    return h * (1.0 + scale) + shift


def context_needed_after(i):
    return any(j % 2 == 0 for j in range(i + 1, DEPTH))


def grid_positions(n):
    rows = n // GRID_W
    row = jnp.repeat(jnp.arange(rows, dtype=jnp.float32), GRID_W)
    col = jnp.tile(jnp.arange(GRID_W, dtype=jnp.float32), rows)
    return row, col


def rope_1d(x, pos):
    half = x.shape[-1] // 2
    inv_freq = ROPE_BASE ** (-jnp.arange(half, dtype=jnp.float32) / half)
    ang = pos[:, None] * inv_freq[None, :]
    cos = jnp.cos(ang)[None, :, None, :].astype(x.dtype)
    sin = jnp.sin(ang)[None, :, None, :].astype(x.dtype)
    x1, x2 = x[..., :half], x[..., half:]
    return jnp.concatenate([x1 * cos - x2 * sin, x2 * cos + x1 * sin], axis=-1)


def rope_2d(x, row, col):
    half = x.shape[-1] // 2
    return jnp.concatenate([rope_1d(x[..., :half], row), rope_1d(x[..., half:], col)], axis=-1)


def softmax_with_sink(s, sink):
    s_all = jnp.concatenate([s, jnp.broadcast_to(sink, s.shape[:-1] + (1,))], axis=-1)
    return jax.nn.softmax(s_all, axis=-1)[..., :-1]


def dense_attention(q, k, v, scale):
    s = jnp.einsum('bqhd,bkhd->bhqk', q, k, preferred_element_type=jnp.float32) * scale
    p = jax.nn.softmax(s, axis=-1).astype(v.dtype)
    return jnp.einsum('bhqk,bkhd->bqhd', p, v)


def blocked_dense_attention(q, k, v, scale):
    B, S, H, dq = q.shape
    nb = S // Q_BLOCK
    q_blocks = q.reshape(B, nb, Q_BLOCK, H, dq).transpose(1, 0, 2, 3, 4)
    out = lax.map(lambda qb: dense_attention(qb, k, v, scale), q_blocks)
    return out.transpose(1, 0, 2, 3, 4).reshape(B, S, H, v.shape[-1])


def mla_keys_values(p, kv_norm_g, w_ukv, pos):
    B, L, _ = p.shape
    c_kv = rmsnorm(p[..., OFF_CKV:OFF_KROPE], kv_norm_g)
    kv = jnp.einsum('blr,rx->blx', c_kv, w_ukv).reshape(B, L, MLA_HEADS, MLA_NOPE + MLA_V)
    k_rope = p[..., OFF_KROPE:OFF_GK].reshape(B, L, 1, MLA_ROPE)
    if pos is not None:
        k_rope = rope_2d(k_rope, *pos)
    k = jnp.concatenate([kv[..., :MLA_NOPE], jnp.broadcast_to(k_rope, (B, L, MLA_HEADS, MLA_ROPE))], axis=-1)
    return k, kv[..., MLA_NOPE:]


def mla_queries(p, q_norm_g, w_uq, pos):
    B, L, _ = p.shape
    c_q = rmsnorm(p[..., OFF_CQ:OFF_GQ], q_norm_g)
    q = jnp.einsum('blr,rx->blx', c_q, w_uq).reshape(B, L, MLA_HEADS, MLA_NOPE + MLA_ROPE)
    if pos is None:
        return q
    return jnp.concatenate([q[..., :MLA_NOPE], rope_2d(q[..., MLA_NOPE:], *pos)], axis=-1)


def gqa_keys_values(p, pos):
    B, L, _ = p.shape
    k = p[..., OFF_GK:OFF_GV].reshape(B, L, GQA_KV_HEADS, GQA_HEAD_DIM)
    v = p[..., OFF_GV:KV_COLS].reshape(B, L, GQA_KV_HEADS, GQA_HEAD_DIM)
    if pos is not None:
        k = rope_2d(k, *pos)
    return k, v


def gqa_queries(p, pos):
    B, L, _ = p.shape
    q = p[..., OFF_GQ:PROJ_COLS].reshape(B, L, GQA_Q_HEADS, GQA_HEAD_DIM)
    return q if pos is None else rope_2d(q, *pos)


def windowed_gqa_with_context(q, k, v, k_ctx, v_ctx, sink):
    B, S, _, _ = q.shape
    nb = S // WIN_BLOCK
    qb = q.reshape(B, nb, WIN_BLOCK, GQA_KV_HEADS, GQA_GROUP, GQA_HEAD_DIM)

    def band(t):
        tp = jnp.pad(t, ((0, 0), (WIN_BLOCK, WIN_BLOCK), (0, 0), (0, 0)))
        tp = tp.reshape(B, nb + 2, WIN_BLOCK, GQA_KV_HEADS, GQA_HEAD_DIM)
        return jnp.concatenate([tp[:, :-2], tp[:, 1:-1], tp[:, 2:]], axis=2)

    kw, vw = band(k), band(v)
    blk = jnp.arange(nb)[:, None, None]
    q_pos = blk * WIN_BLOCK + jnp.arange(WIN_BLOCK)[None, :, None]
    k_pos = (blk - 1) * WIN_BLOCK + jnp.arange(3 * WIN_BLOCK)[None, None, :]
    valid = (jnp.abs(q_pos - k_pos) <= WINDOW) & (k_pos >= 0) & (k_pos < S)
    s_loc = jnp.einsum('bnqhgd,bnkhd->bnhgqk', qb, kw, preferred_element_type=jnp.float32) * GQA_SCALE
    s_loc = jnp.where(valid[None, :, None, None], s_loc, NEG_INF)
    s_ctx = jnp.einsum('bnqhgd,bchd->bnhgqc', qb, k_ctx, preferred_element_type=jnp.float32) * GQA_SCALE
    sink_b = sink.astype(jnp.float32).reshape(1, 1, GQA_KV_HEADS, GQA_GROUP, 1, 1)
    p = softmax_with_sink(jnp.concatenate([s_loc, s_ctx], axis=-1), sink_b).astype(v.dtype)
    n_loc = 3 * WIN_BLOCK
    out = (jnp.einsum('bnhgqk,bnkhd->bnqhgd', p[..., :n_loc], vw)
           + jnp.einsum('bnhgqc,bchd->bnqhgd', p[..., n_loc:], v_ctx))
    return out.reshape(B, S, GQA_Q_HEADS, GQA_HEAD_DIM)


def gqa_context(q, k, v, sink):
    B, C, _, _ = q.shape
    qg = q.reshape(B, C, GQA_KV_HEADS, GQA_GROUP, GQA_HEAD_DIM)
    s = jnp.einsum('bqhgd,bkhd->bhgqk', qg, k, preferred_element_type=jnp.float32) * GQA_SCALE
    sink_b = sink.astype(jnp.float32).reshape(1, GQA_KV_HEADS, GQA_GROUP, 1, 1)
    p = softmax_with_sink(s, sink_b).astype(v.dtype)
    return jnp.einsum('bhgqk,bkhd->bqhgd', p, v).reshape(B, C, GQA_Q_HEADS, GQA_HEAD_DIM)


def merge_heads(mla_out, gqa_out, w_out):
    B, L = mla_out.shape[:2]
    y = jnp.concatenate([mla_out.reshape(B, L, MLA_HEADS * MLA_V),
                         gqa_out.reshape(B, L, GQA_Q_HEADS * GQA_HEAD_DIM)], axis=-1)
    return jnp.einsum('blm,md->bld', y, w_out)


def attention_mixer(h, h_ctx, w_in, q_norm_g, kv_norm_g, w_uq, w_ukv, sink, w_out, need_ctx):
    B, S, _ = h.shape
    pos = grid_positions(S)
    p = jnp.einsum('bld,dp->blp', h, w_in)
    p_ctx = jnp.einsum('bld,dp->blp', h_ctx, w_in if need_ctx else w_in[:, :KV_COLS])
    mk_c, mv_c = mla_keys_values(p_ctx, kv_norm_g, w_ukv, None)
    gk_c, gv_c = gqa_keys_values(p_ctx, None)
    mk, mv = mla_keys_values(p, kv_norm_g, w_ukv, pos)
    gk, gv = gqa_keys_values(p, pos)
    mq = mla_queries(p, q_norm_g, w_uq, pos)
    gq = gqa_queries(p, pos)
    mla_out = blocked_dense_attention(mq, jnp.concatenate([mk_c, mk], axis=1),
                                      jnp.concatenate([mv_c, mv], axis=1), MLA_SCALE)
    gqa_out = windowed_gqa_with_context(gq, gk, gv, gk_c, gv_c, sink)
    y = merge_heads(mla_out, gqa_out, w_out)
    y_ctx = None
    if need_ctx:
        mq_c = mla_queries(p_ctx, q_norm_g, w_uq, None)
        gq_c = gqa_queries(p_ctx, None)
        y_ctx = merge_heads(dense_attention(mq_c, mk_c, mv_c, MLA_SCALE),
                            gqa_context(gq_c, gk_c, gv_c, sink), w_out)
    return y, y_ctx


def hyena_filters(L, ffn_w1, ffn_b1, ffn_w2, ffn_b2, ffn_w3, ffn_b3, sin_freq, ffn_w4):
    f32 = jnp.float32
    t = jnp.linspace(0.0, 1.0, L, dtype=f32)[:, None]
    w = 2.0 * math.pi * jnp.arange(L, dtype=f32)[:, None] / L
    f = jnp.linspace(1e-4, HY_BANDS - 1, HY_BANDS, dtype=f32)[None, :]
    z = jnp.concatenate([t, jnp.cos(f * w), -jnp.sin(f * w)], axis=-1)
    freq = sin_freq.astype(f32)
    a = jnp.sin(freq * (z @ ffn_w1.astype(f32) + ffn_b1.astype(f32)))
    a = jnp.sin(freq * (a @ ffn_w2.astype(f32) + ffn_b2.astype(f32)))
    a = jnp.sin(freq * (a @ ffn_w3.astype(f32) + ffn_b3.astype(f32)))
    hf = (a @ ffn_w4.astype(f32)).reshape(L, HY_ORDER, HY_DIRS, D_MODEL)
    deltas = jnp.linspace(math.log(HY_DECAY_TARGET) / HY_SLOW_DECAY,
                          math.log(HY_DECAY_TARGET) / HY_FAST_DECAY, D_MODEL, dtype=f32)
    decay = jnp.exp(-t * jnp.abs(deltas)[None, :])
    hf = hf * decay[:, None, None, :]
    hf = hf / jnp.sum(jnp.abs(hf), axis=(0, 2), keepdims=True)
    fwd, bwd = hf[:, :, 0], hf[:, :, 1]
    kern = jnp.concatenate([fwd, jnp.zeros((1,) + fwd.shape[1:], f32), bwd[:0:-1]], axis=0)
    return jnp.fft.rfft(kern, axis=0)


def long_conv(u, kern_f, bias):
    L = u.shape[1]
    uf = jnp.fft.rfft(u.astype(jnp.float32), n=2 * L, axis=1)
    y = jnp.fft.irfft(uf * kern_f[None], n=2 * L, axis=1)[:, :L]
    return y + u.astype(jnp.float32) * bias.astype(jnp.float32)


def hyena_mixer(h, w_in, b_in, short_w, short_b, ffn_w1, ffn_b1, ffn_w2, ffn_b2, ffn_w3, ffn_b3,
                sin_freq, ffn_w4, filter_bias, w_out, b_out):
    B, L, D = h.shape
    p = jnp.einsum('bld,dp->blp', h, w_in) + b_in
    p = lax.conv_general_dilated(p, short_w[:, None, :].astype(p.dtype), window_strides=(1,),
                                 padding=[(HY_SHORT // 2, HY_SHORT // 2)],
                                 dimension_numbers=('NWC', 'WIO', 'NWC'),
                                 feature_group_count=p.shape[-1]) + short_b
    v, g1, g2 = jnp.split(p, 3, axis=-1)
    kern_f = hyena_filters(L, ffn_w1, ffn_b1, ffn_w2, ffn_b2, ffn_w3, ffn_b3, sin_freq, ffn_w4)
    z = g1 * long_conv(v, kern_f[:, 0], filter_bias[0])
    z = g2 * long_conv(z, kern_f[:, 1], filter_bias[1])
    return jnp.einsum('bld,de->ble', z.astype(h.dtype), w_out) + b_out


def moe_ffn(h, router_w, router_b, w1, b1, w2, b2):
    B, L, D = h.shape
    t = h.reshape(B * L, D)
    logits = jnp.dot(t, router_w, preferred_element_type=jnp.float32) + router_b.astype(jnp.float32)
    top_v, top_i = lax.top_k(logits, TOP_K)
    top_w = jax.nn.softmax(top_v, axis=-1)
    gates = jnp.einsum('tk,tke->te', top_w, jax.nn.one_hot(top_i, N_EXPERTS, dtype=jnp.float32))
    out = jnp.zeros((B * L, D), jnp.float32)
    for e in range(N_EXPERTS):
        u = jnp.dot(t, w1[e]) + b1[e]
        glu = jnp.minimum(u[:, :EXPERT_FF], SWIGLU_LIMIT)
        lin = jnp.clip(u[:, EXPERT_FF:], -SWIGLU_LIMIT, SWIGLU_LIMIT)
        act = glu * jax.nn.sigmoid(SWIGLU_ALPHA * glu) * (lin + 1.0)
        out = out + gates[:, e:e + 1] * (jnp.dot(act, w2[e]) + b2[e])
    return out.astype(h.dtype).reshape(B, L, D)


def setup_inputs(seed: int = 0) -> dict:
    key = jax.random.key(seed)
    keys = iter(jax.random.split(key, 48))

    def nrm(shape, scale):
        return jax.random.normal(next(keys), shape, jnp.float32) * scale

    D, NA, NH, E, F = D_MODEL, N_ATTN_LAYERS, N_HYENA_LAYERS, N_EXPERTS, EXPERT_FF
    return {
        'x': nrm((BATCH, SEQ, D), 1.0),
        'c': nrm((BATCH, D), 1.0),
        'ctx': nrm((BATCH, CTX_LEN, D), 1.0),
        'c_ctx': nrm((D,), 1.0),
        'mod_w': nrm((DEPTH, D, 6 * D), D ** -0.5),
        'mod_b': nrm((DEPTH, 6 * D), 0.02),
        'norm1_g': 1.0 + nrm((DEPTH, D), 0.02),
        'norm2_g': 1.0 + nrm((DEPTH, D), 0.02),
        'attn_w_in': nrm((NA, D, PROJ_COLS), D ** -0.5),
        'mla_q_norm_g': 1.0 + nrm((NA, MLA_Q_RANK), 0.02),
        'mla_kv_norm_g': 1.0 + nrm((NA, MLA_KV_RANK), 0.02),
        'mla_w_uq': nrm((NA, MLA_Q_RANK, MLA_HEADS * (MLA_NOPE + MLA_ROPE)), MLA_Q_RANK ** -0.5),
        'mla_w_ukv': nrm((NA, MLA_KV_RANK, MLA_HEADS * (MLA_NOPE + MLA_V)), MLA_KV_RANK ** -0.5),
        'gqa_sink': nrm((NA, GQA_Q_HEADS), 0.5),
        'attn_w_out': nrm((NA, MIX_WIDTH, D), MIX_WIDTH ** -0.5),
        'hy_w_in': nrm((NH, D, 3 * D), D ** -0.5),
        'hy_b_in': nrm((NH, 3 * D), 0.02),
        'hy_short_w': nrm((NH, HY_SHORT, 3 * D), HY_SHORT ** -0.5),
        'hy_short_b': nrm((NH, 3 * D), 0.02),
        'hy_ffn_w1': nrm((NH, HY_EMB, HY_FILTER_FF), HY_EMB ** -0.5),
        'hy_ffn_b1': nrm((NH, HY_FILTER_FF), 0.02),
        'hy_ffn_w2': nrm((NH, HY_FILTER_FF, HY_FILTER_FF), HY_FILTER_FF ** -0.5),
        'hy_ffn_b2': nrm((NH, HY_FILTER_FF), 0.02),
        'hy_ffn_w3': nrm((NH, HY_FILTER_FF, HY_FILTER_FF), HY_FILTER_FF ** -0.5),
        'hy_ffn_b3': nrm((NH, HY_FILTER_FF), 0.02),
        'hy_sin_freq': 1.0 + nrm((NH, HY_FILTER_FF), 0.02),
        'hy_ffn_w4': nrm((NH, HY_FILTER_FF, HY_ORDER * HY_DIRS * D), HY_FILTER_FF ** -0.5),
        'hy_filter_bias': nrm((NH, HY_ORDER, D), 1.0),
        'hy_w_out': nrm((NH, D, D), D ** -0.5),
        'hy_b_out': nrm((NH, D), 0.02),
        'moe_router_w': nrm((DEPTH, D, E), D ** -0.5),
        'moe_router_b': nrm((DEPTH, E), 0.01),
        'moe_w1': nrm((DEPTH, E, D, 2 * F), D ** -0.5),
        'moe_b1': nrm((DEPTH, E, 2 * F), 0.02),
        'moe_w2': nrm((DEPTH, E, F, D), F ** -0.5),
        'moe_b2': nrm((DEPTH, E, D), 0.02),
        'final_norm_g': 1.0 + nrm((D,), 0.02),
    }


def reference(x, c, ctx, c_ctx, mod_w, mod_b, norm1_g, norm2_g, attn_w_in, mla_q_norm_g, mla_kv_norm_g,
              mla_w_uq, mla_w_ukv, gqa_sink, attn_w_out, hy_w_in, hy_b_in, hy_short_w, hy_short_b,
              hy_ffn_w1, hy_ffn_b1, hy_ffn_w2, hy_ffn_b2, hy_ffn_w3, hy_ffn_b3, hy_sin_freq, hy_ffn_w4,
              hy_filter_bias, hy_w_out, hy_b_out, moe_router_w, moe_router_b, moe_w1, moe_b1, moe_w2,
              moe_b2, final_norm_g):
    for i in range(DEPTH):
        mod = jnp.einsum('bd,de->be', jax.nn.silu(c), mod_w[i]) + mod_b[i]
        sh1, sc1, gt1, sh2, sc2, gt2 = jnp.split(mod[:, None, :], 6, axis=-1)
        need_ctx = context_needed_after(i)
        if need_ctx or i % 2 == 0:
            n_mod = 6 if need_ctx else 2
            mod_c = jax.nn.silu(c_ctx) @ mod_w[i][:, :n_mod * D_MODEL] + mod_b[i][:n_mod * D_MODEL]
            mod_c = jnp.split(mod_c, n_mod)
            h_ctx = modulate(rmsnorm(ctx, norm1_g[i]), mod_c[0], mod_c[1])
        h = modulate(rmsnorm(x, norm1_g[i]), sh1, sc1)
        j = i // 2
        if i % 2 == 0:
            y, y_ctx = attention_mixer(h, h_ctx, attn_w_in[j], mla_q_norm_g[j], mla_kv_norm_g[j],
                                       mla_w_uq[j], mla_w_ukv[j], gqa_sink[j], attn_w_out[j], need_ctx)
        else:
            hy = (hy_w_in[j], hy_b_in[j], hy_short_w[j], hy_short_b[j], hy_ffn_w1[j], hy_ffn_b1[j],
                  hy_ffn_w2[j], hy_ffn_b2[j], hy_ffn_w3[j], hy_ffn_b3[j], hy_sin_freq[j], hy_ffn_w4[j],
                  hy_filter_bias[j], hy_w_out[j], hy_b_out[j])
            y = hyena_mixer(h, *hy)
            y_ctx = hyena_mixer(h_ctx, *hy) if need_ctx else None
        x = x + gt1 * y
        moe_p = (moe_router_w[i], moe_router_b[i], moe_w1[i], moe_b1[i], moe_w2[i], moe_b2[i])
        x = x + gt2 * moe_ffn(modulate(rmsnorm(x, norm2_g[i]), sh2, sc2), *moe_p)
        if need_ctx:
            ctx = ctx + mod_c[2] * y_ctx
            ctx = ctx + mod_c[5] * moe_ffn(modulate(rmsnorm(ctx, norm2_g[i]), mod_c[3], mod_c[4]), *moe_p)
    return rmsnorm(x, final_norm_g)
```

```python
import functools
import math

import jax
import jax.numpy as jnp
from jax import lax
from jax.experimental import pallas as pl
from jax.experimental.pallas import tpu as pltpu

F32 = jnp.float32
BF16 = jnp.bfloat16

RMS_EPS = 1e-6
ROPE_BASE = 10000.0
GRID_W = 64
NEG_INF = -1e30

MLA_HEADS = 8
MLA_Q_RANK = 256
MLA_KV_RANK = 128
MLA_NOPE = 64
MLA_ROPE = 32
MLA_V = 64
MLA_SCALE = 1.0 / math.sqrt(MLA_NOPE + MLA_ROPE)

GQA_Q_HEADS = 8
GQA_KV_HEADS = 2
GQA_HEAD_DIM = 64
GQA_SCALE = 1.0 / math.sqrt(GQA_HEAD_DIM)
WIN_BLOCK = 128

OFF_CKV = 0
OFF_KROPE = OFF_CKV + MLA_KV_RANK
OFF_GK = OFF_KROPE + MLA_ROPE
OFF_GV = OFF_GK + GQA_KV_HEADS * GQA_HEAD_DIM
KV_COLS = OFF_GV + GQA_KV_HEADS * GQA_HEAD_DIM
OFF_CQ = KV_COLS
OFF_GQ = OFF_CQ + MLA_Q_RANK
PROJ_COLS = OFF_GQ + GQA_Q_HEADS * GQA_HEAD_DIM

HY_BANDS = 16
HY_FILTER_FF = 64
HY_DECAY_TARGET = 1e-2
HY_FAST_DECAY = 0.3
HY_SLOW_DECAY = 1.5

N_EXPERTS = 32
TOP_K = 4
SWIGLU_LIMIT = 7.0
SWIGLU_ALPHA = 1.702

LANES = 128
VMEM_LIMIT_BYTES = 56 * 1024 * 1024

C_CKV = 0
C_CQ = C_CKV + 128
C_GQ = C_CQ + 256
C_GQR = C_GQ + 512
C_GK = C_GQR + 512
C_GKR = C_GK + 512
C_GV = C_GKR + 512
C_KR = C_GV + 256
C_KRR = C_KR + 128
AUG_COLS = C_KRR + 128


def _params(*sem):
    return pltpu.CompilerParams(dimension_semantics=sem, vmem_limit_bytes=VMEM_LIMIT_BYTES)


def _const_spec(shape):
    n = len(shape)
    return pl.BlockSpec(shape, lambda *_: (0,) * n)


def _norm_mod(x, g, sh, sc):
    ms = jnp.mean(x * x, axis=-1, keepdims=True)
    return (x * lax.rsqrt(ms + RMS_EPS) * g) * (1.0 + sc) + sh


def _rms(x, g):
    ms = jnp.mean(x * x, axis=-1, keepdims=True)
    return x * lax.rsqrt(ms + RMS_EPS) * g


def _mod_kernel(cc_ref, w_ref, b_ref, o_ref):
    cc = cc_ref[...]
    s = cc / (1.0 + jnp.exp(-cc))
    w = w_ref[...]
    b = b_ref[...]
    o_ref[0:1, :] = jnp.sum(w * s[:, 0:1], axis=0, keepdims=True) + b
    o_ref[1:2, :] = jnp.sum(w * s[:, 1:2], axis=0, keepdims=True) + b


def _modulation(c, c_ctx, mod_w, mod_b):
    depth, d, n = mod_w.shape
    tn = 1536
    cc = jnp.stack([c.reshape(d), c_ctx.reshape(d)], axis=1)
    return pl.pallas_call(
        _mod_kernel,
        grid=(depth, n // tn),
        in_specs=[
            _const_spec((d, 2)),
            pl.BlockSpec((None, d, tn), lambda l, j: (l, 0, j)),
            pl.BlockSpec((None, 1, tn), lambda l, j: (l, 0, j)),
        ],
        out_specs=pl.BlockSpec((None, 2, tn), lambda l, j: (l, 0, j)),
        out_shape=jax.ShapeDtypeStruct((depth, 2, n), F32),
        compiler_params=_params("parallel", "parallel"),
        name="modulation",
    )(cc, mod_w, mod_b.reshape(depth, 1, n))


def _rot_cols_1d(w):
    half = w.shape[-1] // 2
    return jnp.concatenate([-w[..., half:], w[..., :half]], axis=-1)


def _rot_cols_2d(w):
    half = w.shape[-1] // 2
    return jnp.concatenate([_rot_cols_1d(w[..., :half]), _rot_cols_1d(w[..., half:])], axis=-1)


def _rope_tables(n_lat, n_ctx, dims):
    t = jnp.arange(n_lat, dtype=jnp.int32)
    row = (t // GRID_W).astype(F32)
    col = (t % GRID_W).astype(F32)
    q = dims // 4
    inv = ROPE_BASE ** (-jnp.arange(q, dtype=F32) / q)
    ar = row[:, None] * inv[None, :]
    ac = col[:, None] * inv[None, :]
    cos = jnp.concatenate([jnp.cos(ar), jnp.cos(ar), jnp.cos(ac), jnp.cos(ac)], axis=-1)
    sin = jnp.concatenate([jnp.sin(ar), jnp.sin(ar), jnp.sin(ac), jnp.sin(ac)], axis=-1)
    cos = jnp.concatenate([jnp.ones((n_ctx, dims), F32), cos], axis=0)
    sin = jnp.concatenate([jnp.zeros((n_ctx, dims), F32), sin], axis=0)
    return cos, sin


def _attn_weights(w_in, w_uq, w_ukv):
    d = w_in.shape[0]
    z64 = jnp.zeros((d, 64), F32)
    z32 = jnp.zeros((d, 32), F32)
    ckv = w_in[:, OFF_CKV:OFF_KROPE]
    krope = w_in[:, OFF_KROPE:OFF_GK]
    gk = w_in[:, OFF_GK:OFF_GV]
    gv = w_in[:, OFF_GV:KV_COLS]
    cq = w_in[:, OFF_CQ:OFF_GQ]
    gq = w_in[:, OFF_GQ:PROJ_COLS]
    gq_rot = _rot_cols_2d(gq.reshape(d, GQA_Q_HEADS, GQA_HEAD_DIM)).reshape(d, -1)
    k0, k1 = gk[:, :64], gk[:, 64:]
    k0r, k1r = _rot_cols_2d(k0), _rot_cols_2d(k1)
    gk4 = jnp.concatenate([k0, z64, z64, k0, k1, z64, z64, k1], axis=1)
    gk4r = jnp.concatenate([k0r, z64, z64, k0r, k1r, z64, z64, k1r], axis=1)
    v0, v1 = gv[:, :64], gv[:, 64:]
    gv2 = jnp.concatenate([v0, v0, v1, v1], axis=1)
    kr = jnp.concatenate([z64, krope, z32], axis=1)
    krr = jnp.concatenate([z64, _rot_cols_2d(krope), z32], axis=1)
    w_aug = jnp.concatenate([ckv, cq, gq, gq_rot, gk4, gk4r, gv2, kr, krr], axis=1).astype(BF16)

    r = w_uq.shape[0]
    q3 = w_uq.reshape(r, MLA_HEADS, MLA_NOPE + MLA_ROPE)
    q_main = jnp.pad(q3, ((0, 0), (0, 0), (0, LANES - MLA_NOPE - MLA_ROPE)))
    q_rot = jnp.concatenate([jnp.zeros((r, MLA_HEADS, MLA_NOPE), F32), _rot_cols_2d(q3[..., MLA_NOPE:]),
                             jnp.zeros((r, MLA_HEADS, LANES - MLA_NOPE - MLA_ROPE), F32)], axis=-1)
    wq = jnp.concatenate([q_main.reshape(r, -1), q_rot.reshape(r, -1)], axis=1).astype(BF16)

    rk = w_ukv.shape[0]
    kv3 = w_ukv.reshape(rk, MLA_HEADS, MLA_NOPE + MLA_V)
    k_part = jnp.pad(kv3[..., :MLA_NOPE], ((0, 0), (0, 0), (0, LANES - MLA_NOPE))).reshape(rk, -1)
    v_part = kv3[..., MLA_NOPE:].reshape(rk, -1)
    wkv = jnp.concatenate([k_part, v_part], axis=1).astype(BF16)
    return w_aug, wq, wkv


def _attn_prep_kernel(x_ref, ctx_ref, g_ref, sh_ref, sc_ref, w_ref, gq_ref, gkv_ref, wq_ref, wkv_ref,
                      cm_ref, sm_ref, cg_ref, sg_ref,
                      mq_ref, mk_ref, mv_ref, gqo_ref, gko_ref, gvo_ref):
    i = pl.program_id(0)
    xin = jnp.where(i == 0, ctx_ref[...], x_ref[...])
    h = _norm_mod(xin, g_ref[...], sh_ref[...], sc_ref[...])
    p = jnp.dot(h.astype(BF16), w_ref[...], preferred_element_type=F32)

    ckv = _rms(p[:, C_CKV:C_CKV + 128], gkv_ref[...])
    kv = jnp.dot(ckv.astype(BF16), wkv_ref[...], preferred_element_type=F32)
    cq = _rms(p[:, C_CQ:C_CQ + 256], gq_ref[...])
    qa = jnp.dot(cq.astype(BF16), wq_ref[...], preferred_element_type=F32)

    cm, sm = cm_ref[...], sm_ref[...]
    kr = p[:, C_KR:C_KR + 128] * cm + p[:, C_KRR:C_KRR + 128] * sm
    for hd in range(MLA_HEADS):
        lo = hd * LANES
        qh = qa[:, lo:lo + LANES] * cm + qa[:, 1024 + lo:1024 + lo + LANES] * sm
        mq_ref[hd] = (qh * MLA_SCALE).astype(BF16)
        mk_ref[hd] = (kv[:, lo:lo + LANES] + kr).astype(BF16)
    mv_ref[...] = kv[:, 1024:1536].astype(BF16)

    cg, sg = cg_ref[...], sg_ref[...]
    for gI in range(4):
        lo = gI * LANES
        gq = p[:, C_GQ + lo:C_GQ + lo + LANES] * cg + p[:, C_GQR + lo:C_GQR + lo + LANES] * sg
        gqo_ref[:, lo:lo + LANES] = (gq * GQA_SCALE).astype(BF16)
        gk = p[:, C_GK + lo:C_GK + lo + LANES] * cg + p[:, C_GKR + lo:C_GKR + lo + LANES] * sg
        gko_ref[:, lo:lo + LANES] = gk.astype(BF16)
    gvo_ref[...] = p[:, C_GV:C_GV + 256].astype(BF16)


def _attn_prep(x, ctx, g, sh2, sc2, w_aug, gq, gkv, wq, wkv, cm, sm, cg, sg):
    s, d = x.shape
    c = ctx.shape[0]
    tm = c
    nt = s // tm + 1
    tot = s + c
    lat = lambda i: (jnp.maximum(i - 1, 0), 0)
    row = lambda i: (i, 0)
    return pl.pallas_call(
        _attn_prep_kernel,
        grid=(nt,),
        in_specs=[
            pl.BlockSpec((tm, d), lat),
            _const_spec((c, d)),
            _const_spec((1, d)),
            pl.BlockSpec((None, 1, d), lambda i: (jnp.minimum(i, 1), 0, 0)),
            pl.BlockSpec((None, 1, d), lambda i: (jnp.minimum(i, 1), 0, 0)),
            _const_spec(w_aug.shape),
            _const_spec(gq.shape),
            _const_spec(gkv.shape),
            _const_spec(wq.shape),
            _const_spec(wkv.shape),
            pl.BlockSpec((tm, LANES), row),
            pl.BlockSpec((tm, LANES), row),
            pl.BlockSpec((tm, LANES), row),
            pl.BlockSpec((tm, LANES), row),
        ],
        out_specs=[
            pl.BlockSpec((MLA_HEADS, tm, LANES), lambda i: (0, jnp.maximum(i - 1, 0), 0)),
            pl.BlockSpec((MLA_HEADS, tm, LANES), lambda i: (0, i, 0)),
            pl.BlockSpec((tm, 512), row),
            pl.BlockSpec((tm, 512), lat),
            pl.BlockSpec((tm, 512), row),
            pl.BlockSpec((tm, 256), row),
        ],
        out_shape=[
            jax.ShapeDtypeStruct((MLA_HEADS, s, LANES), BF16),
            jax.ShapeDtypeStruct((MLA_HEADS, tot, LANES), BF16),
            jax.ShapeDtypeStruct((tot, 512), BF16),
            jax.ShapeDtypeStruct((s, 512), BF16),
            jax.ShapeDtypeStruct((tot, 512), BF16),
            jax.ShapeDtypeStruct((tot, 256), BF16),
        ],
        compiler_params=_params("arbitrary"),
        name="attn_prep",
    )(x, ctx, g, sh2, sc2, w_aug, gq, gkv, wq, wkv, cm, sm, cg, sg)


def _mla_kernel(q_ref, k_ref, v_ref, o_ref, m_sc, l_sc, acc_sc):
    j = pl.program_id(2)

    @pl.when(j == 0)
    def _():
        m_sc[...] = jnp.full(m_sc.shape, -jnp.inf, F32)
        l_sc[...] = jnp.zeros(l_sc.shape, F32)
        acc_sc[...] = jnp.zeros(acc_sc.shape, F32)

    v = v_ref[...]
    for hh in range(2):
        s = lax.dot_general(q_ref[hh], k_ref[hh], (((1,), (1,)), ((), ())),
                            preferred_element_type=F32)
        m_prev = m_sc[hh]
        m_new = jnp.maximum(m_prev, jnp.max(s, axis=-1, keepdims=True))
        alpha = jnp.exp(m_prev - m_new)
        p = jnp.exp(s - m_new)
        l_sc[hh] = alpha * l_sc[hh] + jnp.sum(p, axis=-1, keepdims=True)
        acc_sc[hh] = alpha * acc_sc[hh] + jnp.dot(p.astype(BF16), v, preferred_element_type=F32)
        m_sc[hh] = m_new

    @pl.when(j == pl.num_programs(2) - 1)
    def _():
        lane = lax.broadcasted_iota(jnp.int32, acc_sc.shape[1:], 1)
        o0 = acc_sc[0] / l_sc[0]
        o1 = acc_sc[1] / l_sc[1]
        o_ref[...] = jnp.where(lane < MLA_V, o0, o1).astype(o_ref.dtype)


def _mla_attention(mq, mk, mv, tq, tk):
    _, s, _ = mq.shape
    tot = mk.shape[1]
    return pl.pallas_call(
        _mla_kernel,
        grid=(MLA_HEADS // 2, s // tq, tot // tk),
        in_specs=[
            pl.BlockSpec((2, tq, LANES), lambda h, i, j: (h, i, 0)),
            pl.BlockSpec((2, tk, LANES), lambda h, i, j: (h, j, 0)),
            pl.BlockSpec((tk, LANES), lambda h, i, j: (j, h)),
        ],
        out_specs=pl.BlockSpec((tq, LANES), lambda h, i, j: (i, h)),
        out_shape=jax.ShapeDtypeStruct((s, MLA_HEADS * MLA_V), BF16),
        scratch_shapes=[
            pltpu.VMEM((2, tq, 1), F32),
            pltpu.VMEM((2, tq, 1), F32),
            pltpu.VMEM((2, tq, LANES), F32),
        ],
        compiler_params=_params("parallel", "parallel", "arbitrary"),
        name="mla_flash",
    )(mq, mk, mv)


def _gqa_kernel(sink_ref, q_ref, kp_ref, ko_ref, kn_ref, kc_ref, vp_ref, vo_ref, vn_ref, vc_ref, o_ref):
    n = pl.program_id(0)
    nb = pl.num_programs(0)
    wb = WIN_BLOCK
    r = lax.broadcasted_iota(jnp.int32, (wb, wb), 0)
    c = lax.broadcasted_iota(jnp.int32, (wb, wb), 1)
    ok_prev = c >= r + jnp.where(n >= 1, 0, wb)
    ok_next = c + jnp.where(n <= nb - 2, 0, wb) <= r
    lane = lax.broadcasted_iota(jnp.int32, (wb, LANES), 1)
    dn = (((1,), (1,)), ((), ()))
    for pr in range(GQA_Q_HEADS // 2):
        kvh = pr // 2
        q = q_ref[:, pr * LANES:(pr + 1) * LANES]
        vlo = kvh * LANES
        outs = []
        for var in range(2):
            hd = 2 * pr + var
            klo = (2 * kvh + var) * LANES
            s_p = lax.dot_general(q, kp_ref[:, klo:klo + LANES], dn, preferred_element_type=F32)
            s_o = lax.dot_general(q, ko_ref[:, klo:klo + LANES], dn, preferred_element_type=F32)
            s_n = lax.dot_general(q, kn_ref[:, klo:klo + LANES], dn, preferred_element_type=F32)
            s_c = lax.dot_general(q, kc_ref[:, klo:klo + LANES], dn, preferred_element_type=F32)
            s_p = jnp.where(ok_prev, s_p, NEG_INF)
            s_n = jnp.where(ok_next, s_n, NEG_INF)
            sink = sink_ref[hd]
            m = jnp.maximum(jnp.maximum(jnp.max(s_p, axis=-1, keepdims=True),
                                        jnp.max(s_o, axis=-1, keepdims=True)),
                            jnp.maximum(jnp.max(s_n, axis=-1, keepdims=True),
                                        jnp.max(s_c, axis=-1, keepdims=True)))
            m = jnp.maximum(m, sink)
            e_p, e_o, e_n, e_c = jnp.exp(s_p - m), jnp.exp(s_o - m), jnp.exp(s_n - m), jnp.exp(s_c - m)
            den = (jnp.sum(e_p, axis=-1, keepdims=True) + jnp.sum(e_o, axis=-1, keepdims=True)
                   + jnp.sum(e_n, axis=-1, keepdims=True) + jnp.sum(e_c, axis=-1, keepdims=True)
                   + jnp.exp(sink - m))
            o = (jnp.dot(e_p.astype(BF16), vp_ref[:, vlo:vlo + LANES], preferred_element_type=F32)
                 + jnp.dot(e_o.astype(BF16), vo_ref[:, vlo:vlo + LANES], preferred_element_type=F32)
                 + jnp.dot(e_n.astype(BF16), vn_ref[:, vlo:vlo + LANES], preferred_element_type=F32)
                 + jnp.dot(e_c.astype(BF16), vc_ref[:, vlo:vlo + LANES], preferred_element_type=F32))
            outs.append(o / den)
        o_ref[:, pr * LANES:(pr + 1) * LANES] = jnp.where(lane < GQA_HEAD_DIM, outs[0], outs[1]).astype(o_ref.dtype)


def _gqa_attention(sink, gq, gk, gv, n_ctx):
    s = gq.shape[0]
    wb = WIN_BLOCK
    nb = s // wb
    cb = n_ctx // wb
    prev = lambda n: (n + cb - 1, 0)
    own = lambda n: (n + cb, 0)
    nxt = lambda n: (jnp.minimum(n + cb + 1, nb + cb - 1), 0)
    return pl.pallas_call(
        _gqa_kernel,
        grid=(nb,),
        in_specs=[
            pl.BlockSpec(memory_space=pltpu.SMEM),
            pl.BlockSpec((wb, 512), lambda n: (n, 0)),
            pl.BlockSpec((wb, 512), prev),
            pl.BlockSpec((wb, 512), own),
            pl.BlockSpec((wb, 512), nxt),
            _const_spec((n_ctx, 512)),
            pl.BlockSpec((wb, 256), prev),
            pl.BlockSpec((wb, 256), own),
            pl.BlockSpec((wb, 256), nxt),
            _const_spec((n_ctx, 256)),
        ],
        out_specs=pl.BlockSpec((wb, 512), lambda n: (n, 0)),
        out_shape=jax.ShapeDtypeStruct((s, 512), BF16),
        compiler_params=_params("parallel"),
        name="gqa_window",
    )(sink, gq, gk, gk, gk, gk, gv, gv, gv, gv)


def _resid_proj2_kernel(a_ref, b_ref, wa_ref, wb_ref, x_ref, gt_ref, o_ref):
    y = (jnp.dot(a_ref[...], wa_ref[...], preferred_element_type=F32)
         + jnp.dot(b_ref[...], wb_ref[...], preferred_element_type=F32))
    o_ref[...] = x_ref[...] + gt_ref[...] * y


def _resid_proj2(a, b, wa, wb, x, gate, tm):
    s, d = x.shape
    row = lambda i: (i, 0)
    return pl.pallas_call(
        _resid_proj2_kernel,
        grid=(s // tm,),
        in_specs=[pl.BlockSpec((tm, a.shape[1]), row), pl.BlockSpec((tm, b.shape[1]), row),
                  _const_spec(wa.shape), _const_spec(wb.shape),
                  pl.BlockSpec((tm, d), row), _const_spec((1, d))],
        out_specs=pl.BlockSpec((tm, d), row),
        out_shape=jax.ShapeDtypeStruct((s, d), F32),
        compiler_params=_params("parallel"),
        name="attn_out_proj",
    )(a, b, wa, wb, x, gate)


def _resid_proj1_kernel(a_ref, w_ref, bias_ref, x_ref, gt_ref, o_ref):
    y = jnp.dot(a_ref[...], w_ref[...], preferred_element_type=F32) + bias_ref[...]
    o_ref[...] = x_ref[...] + gt_ref[...] * y


def _resid_proj1(a, w, bias, x, gate, tm):
    s, d = x.shape
    row = lambda i: (i, 0)
    return pl.pallas_call(
        _resid_proj1_kernel,
        grid=(s // tm,),
        in_specs=[pl.BlockSpec((tm, a.shape[1]), row), _const_spec(w.shape), _const_spec((1, d)),
                  pl.BlockSpec((tm, d), row), _const_spec((1, d))],
        out_specs=pl.BlockSpec((tm, d), row),
        out_shape=jax.ShapeDtypeStruct((s, d), F32),
        compiler_params=_params("parallel"),
        name="hyena_out_proj",
    )(a, w, bias, x, gate)


def _router_kernel(x_ref, g_ref, sh_ref, sc_ref, rw_ref, rb_ref, t_ref, gates_ref):
    t = _norm_mod(x_ref[...], g_ref[...], sh_ref[...], sc_ref[...])
    t_ref[...] = t.astype(BF16)
    logits = jnp.dot(t, rw_ref[...], preferred_element_type=F32,
                     precision=lax.Precision.HIGHEST) + rb_ref[...]
    lane = lax.broadcasted_iota(jnp.int32, logits.shape, 1)
    work = logits
    vals, hots = [], []
    for _ in range(TOP_K):
        m = jnp.max(work, axis=-1, keepdims=True)
        idx = jnp.min(jnp.where(work == m, lane, N_EXPERTS), axis=-1, keepdims=True)
        hot = lane == idx
        vals.append(m)
        hots.append(hot)
        work = jnp.where(hot, -jnp.inf, work)
    es = [jnp.exp(v - vals[0]) for v in vals]
    den = es[0] + es[1] + es[2] + es[3]
    gates = jnp.zeros(logits.shape, F32)
    for e, hot in zip(es, hots):
        gates = gates + jnp.where(hot, e / den, 0.0)
    gates_ref[...] = gates


def _router(x, g, sh, sc, rw, rb, tm):
    s, d = x.shape
    e = rw.shape[1]
    row = lambda i: (i, 0)
    return pl.pallas_call(
        _router_kernel,
        grid=(s // tm,),
        in_specs=[pl.BlockSpec((tm, d), row), _const_spec((1, d)), _const_spec((1, d)), _const_spec((1, d)),
                  _const_spec((d, e)), _const_spec((1, e))],
        out_specs=[pl.BlockSpec((tm, d), row), pl.BlockSpec((tm, e), row)],
        out_shape=[jax.ShapeDtypeStruct((s, d), BF16), jax.ShapeDtypeStruct((s, e), F32)],
        compiler_params=_params("parallel"),
        name="moe_router",
    )(x, g, sh, sc, rw, rb)


def _moe_dense_kernel(t_ref, gates_ref, w1_ref, b1_ref, w2_ref, b2_ref, x_ref, gt_ref, fg_ref, o_ref, acc_sc,
                      *, ff, final_norm):
    e = pl.program_id(1)

    @pl.when(e == 0)
    def _():
        acc_sc[...] = jnp.zeros(acc_sc.shape, F32)

    u = jnp.dot(t_ref[...], w1_ref[...], preferred_element_type=F32) + b1_ref[...]
    glu = jnp.minimum(u[:, :ff], SWIGLU_LIMIT)
    lin = jnp.clip(u[:, ff:], -SWIGLU_LIMIT, SWIGLU_LIMIT)
    act = glu / (1.0 + jnp.exp(-SWIGLU_ALPHA * glu)) * (lin + 1.0)
    y = jnp.dot(act.astype(BF16), w2_ref[...], preferred_element_type=F32) + b2_ref[...]
    gates = gates_ref[...]
    lane = lax.broadcasted_iota(jnp.int32, gates.shape, 1)
    gcol = jnp.sum(jnp.where(lane == e, gates, 0.0), axis=-1, keepdims=True)
    acc_sc[...] += gcol * y

    @pl.when(e == pl.num_programs(1) - 1)
    def _():
        xo = x_ref[...] + gt_ref[...] * acc_sc[...]
        if final_norm:
            xo = _rms(xo, fg_ref[...])
        o_ref[...] = xo


def _moe_dense(t, gates, w1, b1, w2, b2, x, gate, final_g, tm, final_norm):
    s, d = x.shape
    ne, _, ff2 = w1.shape
    ff = ff2 // 2
    row = lambda i, e: (i, 0)
    return pl.pallas_call(
        functools.partial(_moe_dense_kernel, ff=ff, final_norm=final_norm),
        grid=(s // tm, ne),
        in_specs=[
            pl.BlockSpec((tm, d), row),
            pl.BlockSpec((tm, ne), row),
            pl.BlockSpec((None, d, ff2), lambda i, e: (e, 0, 0)),
            pl.BlockSpec((None, 1, ff2), lambda i, e: (e, 0, 0)),
            pl.BlockSpec((None, ff, d), lambda i, e: (e, 0, 0)),
            pl.BlockSpec((None, 1, d), lambda i, e: (e, 0, 0)),
            pl.BlockSpec((tm, d), row),
            _const_spec((1, d)),
            _const_spec((1, d)),
        ],
        out_specs=pl.BlockSpec((tm, d), row),
        out_shape=jax.ShapeDtypeStruct((s, d), F32),
        scratch_shapes=[pltpu.VMEM((tm, d), F32)],
        compiler_params=_params("parallel", "arbitrary"),
        name="moe_experts",
    )(t, gates, w1, b1.reshape(ne, 1, ff2), w2, b2.reshape(ne, 1, d), x, gate, final_g)


def _moe_layer(x, g, sh, sc, gate, rw, rb, w1, b1, w2, b2, final_g, final_norm):
    t, gates = _router(x, g, sh, sc, rw, rb.reshape(1, -1), tm=512)
    return _moe_dense(t, gates, w1.astype(BF16), b1, w2.astype(BF16), b2, x, gate, final_g,
                      tm=512, final_norm=final_norm)


def _hy_in_kernel(xp_ref, x_ref, xn_ref, g_ref, sh_ref, sc_ref, w_ref, b_ref, sw_ref, sb_ref,
                  v_ref, g1_ref, g2_ref, *, tm, seq):
    i = pl.program_id(0)
    d = x_ref.shape[1]
    xe = jnp.concatenate([xp_ref[...], x_ref[...], xn_ref[...]], axis=0)
    h = _norm_mod(xe, g_ref[...], sh_ref[...], sc_ref[...])
    p = jnp.dot(h.astype(BF16), w_ref[...], preferred_element_type=F32) + b_ref[...]
    row = i * tm - 8 + lax.broadcasted_iota(jnp.int32, (tm + 16, 1), 0)
    p = jnp.where((row >= 0) & (row < seq), p, 0.0)
    pm = pltpu.roll(p, 1, 0)[8:8 + tm]
    pc = p[8:8 + tm]
    pn = pltpu.roll(p, tm + 15, 0)[8:8 + tm]
    sw = sw_ref[...]
    out = sw[0:1] * pm + sw[1:2] * pc + sw[2:3] * pn + sb_ref[...]
    v_ref[...] = out[:, :d].astype(BF16)
    g1_ref[...] = out[:, d:2 * d].astype(BF16)
    g2_ref[...] = out[:, 2 * d:].astype(BF16)


def _hy_in(x, g, sh, sc, w, b, sw, sb, tm):
    s, d = x.shape
    n3 = w.shape[1]
    r8 = tm // 8
    nblk8 = s // 8
    row = lambda i: (i, 0)
    out = jax.ShapeDtypeStruct((s, d), BF16)
    return pl.pallas_call(
        functools.partial(_hy_in_kernel, tm=tm, seq=s),
        grid=(s // tm,),
        in_specs=[
            pl.BlockSpec((8, d), lambda i: (jnp.maximum(i * r8 - 1, 0), 0)),
            pl.BlockSpec((tm, d), row),
            pl.BlockSpec((8, d), lambda i: (jnp.minimum((i + 1) * r8, nblk8 - 1), 0)),
            _const_spec((1, d)), _const_spec((1, d)), _const_spec((1, d)),
            _const_spec((d, n3)), _const_spec((1, n3)), _const_spec((3, n3)), _const_spec((1, n3)),
        ],
        out_specs=[pl.BlockSpec((tm, d), row)] * 3,
        out_shape=[out, out, out],
        compiler_params=_params("parallel"),
        name="hyena_in_proj",
    )(x, x, x, g, sh, sc, w, b, sw, sb)


def _hy_filter_kernel(z_ref, w1_ref, b1_ref, w2_ref, b2_ref, w3_ref, b3_ref, fr_ref, w4_ref, dl_ref,
                      hf_ref, nrm_ref):
    i = pl.program_id(0)
    hp = lax.Precision.HIGHEST
    z = z_ref[...]
    fr = fr_ref[...]
    a = jnp.sin(fr * (jnp.dot(z, w1_ref[...], preferred_element_type=F32, precision=hp) + b1_ref[...]))
    a = jnp.sin(fr * (jnp.dot(a, w2_ref[...], preferred_element_type=F32, precision=hp) + b2_ref[...]))
    a = jnp.sin(fr * (jnp.dot(a, w3_ref[...], preferred_element_type=F32, precision=hp) + b3_ref[...]))
    hf = jnp.dot(a.astype(BF16), w4_ref[...], preferred_element_type=F32)
    decay = jnp.exp(-z[:, 0:1] * jnp.abs(dl_ref[...]))
    d = decay.shape[1]
    cols = []
    for q in range(4):
        cols.append(hf[:, q * d:(q + 1) * d] * decay)
    hf = jnp.concatenate(cols, axis=1)

    @pl.when(i == 0)
    def _():
        nrm_ref[...] = jnp.zeros(nrm_ref.shape, F32)

    nrm_ref[...] += jnp.sum(jnp.abs(hf), axis=0, keepdims=True)
    t0 = (i == 0) & (lax.broadcasted_iota(jnp.int32, hf.shape, 0) == 0)
    col = lax.broadcasted_iota(jnp.int32, hf.shape, 1)
    bwd = ((col >= d) & (col < 2 * d)) | (col >= 3 * d)
    hf_ref[...] = jnp.where(t0 & bwd, 0.0, hf).astype(BF16)


def _hy_filter(z, w1, b1, w2, b2, w3, b3, fr, w4, deltas, tl):
    seq = z.shape[0]
    n4 = w4.shape[1]
    ff = w2.shape[0]
    return pl.pallas_call(
        _hy_filter_kernel,
        grid=(seq // tl,),
        in_specs=[pl.BlockSpec((tl, z.shape[1]), lambda i: (i, 0)),
                  _const_spec(w1.shape), _const_spec((1, ff)), _const_spec(w2.shape), _const_spec((1, ff)),
                  _const_spec(w3.shape), _const_spec((1, ff)), _const_spec((1, ff)), _const_spec(w4.shape),
                  _const_spec(deltas.shape)],
        out_specs=[pl.BlockSpec((tl, n4), lambda i: (i, 0)), _const_spec((1, n4))],
        out_shape=[jax.ShapeDtypeStruct((seq, n4), BF16), jax.ShapeDtypeStruct((1, n4), F32)],
        compiler_params=_params("arbitrary"),
        name="hyena_filter_mlp",
    )(z, w1, b1, w2, b2, w3, b3, fr, w4, deltas)


def _dft_a_kernel(f_ref, u_ref, o_ref):
    o_ref[...] = jnp.dot(f_ref[...], u_ref[...], preferred_element_type=F32).astype(o_ref.dtype)


def _dft_a(f1, u2, tn):
    m, k = f1.shape
    cols = u2.shape[1]
    return pl.pallas_call(
        _dft_a_kernel,
        grid=(cols // tn,),
        in_specs=[_const_spec((m, k)), pl.BlockSpec((k, tn), lambda j: (0, j))],
        out_specs=pl.BlockSpec((m, tn), lambda j: (0, j)),
        out_shape=jax.ShapeDtypeStruct((m, cols), BF16),
        compiler_params=_params("parallel"),
        name="dft_stage_a",
    )(f1, u2)


def _spectrum_kernel(wf_ref, a_ref, inv_ref, k_ref, *, kb, n2):
    inv = inv_ref[...]
    d = inv.shape[1]
    for b in range(kb):
        a = a_ref[:, b].reshape(2 * n2, 2 * d)
        x = jnp.dot(wf_ref[b], a, preferred_element_type=F32)
        xr, xi = x[:n2], x[n2:]
        k_ref[0, b] = (xr[:, :d] + xr[:, d:]) * inv
        k_ref[1, b] = (xi[:, :d] - xi[:, d:]) * inv


def _filter_spectrum(wf, a4, inv_norm, kb):
    _, kh, n2, c4 = a4.shape
    d = c4 // 4
    return pl.pallas_call(
        functools.partial(_spectrum_kernel, kb=kb, n2=n2),
        grid=(2, kh // kb),
        in_specs=[pl.BlockSpec((kb, 2 * n2, 2 * n2), lambda o, j: (j, 0, 0)),
                  pl.BlockSpec((2, kb, n2, 2 * d), lambda o, j: (0, j, 0, o)),
                  pl.BlockSpec((None, 1, d), lambda o, j: (o, 0, 0))],
        out_specs=pl.BlockSpec((None, 2, kb, n2, d), lambda o, j: (o, 0, j, 0, 0)),
        out_shape=jax.ShapeDtypeStruct((2, 2, kh, n2, d), F32),
        compiler_params=_params("parallel", "parallel"),
        name="filter_spectrum",
    )(wf, a4, inv_norm)


def _dft_mid_kernel(wf_ref, wi_ref, a_ref, k_ref, o_ref, *, kb, n2):
    c = a_ref.shape[-1]
    for b in range(kb):
        a = a_ref[:, b].reshape(2 * n2, c)
        x = jnp.dot(wf_ref[b], a, preferred_element_type=F32)
        xr, xi = x[:n2], x[n2:]
        kr, ki = k_ref[0, b], k_ref[1, b]
        y = jnp.concatenate([xr * kr - xi * ki, xr * ki + xi * kr], axis=0).astype(BF16)
        bb = jnp.dot(wi_ref[b], y, preferred_element_type=F32)
        o_ref[:, b] = bb.reshape(2, n2, c).astype(o_ref.dtype)


def _dft_mid(wf, wi, a4, kspec, order, kb):
    _, kh, n2, c = a4.shape
    return pl.pallas_call(
        functools.partial(_dft_mid_kernel, kb=kb, n2=n2),
        grid=(kh // kb,),
        in_specs=[pl.BlockSpec((kb, 2 * n2, 2 * n2), lambda j: (j, 0, 0)),
                  pl.BlockSpec((kb, 2 * n2, 2 * n2), lambda j: (j, 0, 0)),
                  pl.BlockSpec((2, kb, n2, c), lambda j: (0, j, 0, 0)),
                  pl.BlockSpec((None, 2, kb, n2, c), lambda j: (order, 0, j, 0, 0))],
        out_specs=pl.BlockSpec((2, kb, n2, c), lambda j: (0, j, 0, 0)),
        out_shape=jax.ShapeDtypeStruct(a4.shape, BF16),
        compiler_params=_params("parallel"),
        name="dft_middle",
    )(wf, wi, a4, kspec)


def _dft_c_kernel(f_ref, b_ref, u_ref, g_ref, bias_ref, o_ref):
    y = jnp.dot(f_ref[...], b_ref[...], preferred_element_type=F32)
    u = u_ref[...].astype(F32)
    o_ref[...] = (g_ref[...].astype(F32) * (y + u * bias_ref[...])).astype(o_ref.dtype)


def _dft_c(f3, b2, u2, g2, bias_t, tn):
    m, k = f3.shape
    cols = b2.shape[1]
    col = lambda j: (0, j)
    return pl.pallas_call(
        _dft_c_kernel,
        grid=(cols // tn,),
        in_specs=[_const_spec((m, k)), pl.BlockSpec((k, tn), col), pl.BlockSpec((m, tn), col),
                  pl.BlockSpec((m, tn), col), _const_spec((1, tn))],
        out_specs=pl.BlockSpec((m, tn), col),
        out_shape=jax.ShapeDtypeStruct((m, cols), BF16),
        compiler_params=_params("parallel"),
        name="dft_stage_c",
    )(f3, b2, u2, g2, bias_t)


def _dft_constants(seq):
    n = 2 * seq
    n2 = int(round(math.sqrt(seq)))
    n1 = n // n2
    assert n1 * n2 == n and n1 == 2 * n2
    kreal = n1 // 2 + 1
    kh = ((kreal + 7) // 8) * 8
    k1 = jnp.arange(kh, dtype=jnp.int32)
    live = (k1 < kreal)
    nn1 = jnp.arange(n1 // 2, dtype=jnp.int32)
    ph = ((k1[:, None] * nn1[None, :]) % n1).astype(F32) * (2.0 * math.pi / n1)
    lv = live[:, None].astype(F32)
    f1 = jnp.concatenate([jnp.cos(ph) * lv, -jnp.sin(ph) * lv], axis=0).astype(BF16)
    kk2 = jnp.arange(n2, dtype=jnp.int32)
    nn2 = jnp.arange(n2, dtype=jnp.int32)
    num = (k1[:, None, None] * nn2[None, None, :] + n1 * kk2[None, :, None] * nn2[None, None, :]) % n
    ang = num.astype(F32) * (2.0 * math.pi / n)
    mr, mi = jnp.cos(ang), -jnp.sin(ang)
    wf = jnp.concatenate([jnp.concatenate([mr, -mi], axis=2),
                          jnp.concatenate([mi, mr], axis=2)], axis=1).astype(BF16)
    gr, gi = jnp.swapaxes(mr, 1, 2), -jnp.swapaxes(mi, 1, 2)
    wi = jnp.concatenate([jnp.concatenate([gr, -gi], axis=2),
                          jnp.concatenate([gi, gr], axis=2)], axis=1).astype(BF16)
    ck = jnp.where((k1 == 0) | (k1 == n1 // 2), 1.0, 2.0) * live.astype(F32) / n
    ph3 = ((nn1[:, None] * k1[None, :]) % n1).astype(F32) * (2.0 * math.pi / n1)
    f3 = jnp.concatenate([jnp.cos(ph3) * ck[None, :], -jnp.sin(ph3) * ck[None, :]], axis=1).astype(BF16)
    return n1, n2, kh, f1, wf, wi, f3


def _hyena_filter_inputs(seq):
    t = jnp.linspace(0.0, 1.0, seq, dtype=F32)[:, None]
    w = 2.0 * math.pi * jnp.arange(seq, dtype=F32)[:, None] / seq
    f = jnp.linspace(1e-4, HY_BANDS - 1, HY_BANDS, dtype=F32)[None, :]
    z = jnp.concatenate([t, jnp.cos(f * w), -jnp.sin(f * w)], axis=-1)
    return jnp.pad(z, ((0, 0), (0, HY_FILTER_FF - z.shape[1])))


def _hyena_mixer(x, g, sh, sc, gate, w_in, b_in, short_w, short_b, fw1, fb1, fw2, fb2, fw3, fb3,
                 sin_freq, fw4, filter_bias, w_out, b_out):
    seq, d = x.shape
    n1, n2, kh, f1, wf, wi, f3 = _dft_constants(seq)
    kb = 8
    v, g1, g2 = _hy_in(x, g, sh, sc, w_in.astype(BF16), b_in.reshape(1, -1), short_w, short_b.reshape(1, -1),
                       tm=256)

    z = _hyena_filter_inputs(seq)
    fw1p = jnp.pad(fw1, ((0, HY_FILTER_FF - fw1.shape[0]), (0, 0)))
    deltas = jnp.linspace(math.log(HY_DECAY_TARGET) / HY_SLOW_DECAY,
                          math.log(HY_DECAY_TARGET) / HY_FAST_DECAY, d, dtype=F32)[None, :]
    hf, nrm = _hy_filter(z, fw1p, fb1.reshape(1, -1), fw2, fb2.reshape(1, -1), fw3, fb3.reshape(1, -1),
                         sin_freq.reshape(1, -1), fw4.astype(BF16), deltas, tl=512)
    nrm = nrm.reshape(2, 2, d)
    inv_norm = (1.0 / (nrm[:, 0] + nrm[:, 1])).reshape(2, 1, d)
    half = n1 // 2
    fa = _dft_a(f1, hf.reshape(half, n2 * 4 * d), tn=4096)
    kspec = _filter_spectrum(wf, fa.reshape(2, kh, n2, 4 * d), inv_norm, kb)

    def conv(u, gt, order):
        a = _dft_a(f1, u.reshape(half, n2 * d), tn=4096)
        b = _dft_mid(wf, wi, a.reshape(2, kh, n2, d), kspec, order, kb)
        tn = 2 * d
        bias_t = jnp.tile(filter_bias[order].reshape(1, d), (1, tn // d))
        zz = _dft_c(f3, b.reshape(2 * kh, n2 * d), u.reshape(half, n2 * d), gt.reshape(half, n2 * d), bias_t, tn)
        return zz.reshape(seq, d)

    z1 = conv(v, g1, 0)
    z2 = conv(z1, g2, 1)
    return _resid_proj1(z2, w_out.astype(BF16), b_out.reshape(1, -1), x, gate, tm=512)


def _attention_mixer(x, ctx, g, sh, sc, sh_c, sc_c, gate, w_in, q_norm_g, kv_norm_g, w_uq, w_ukv, sink, w_out):
    s, d = x.shape
    n_ctx = ctx.shape[0]
    w_aug, wq, wkv = _attn_weights(w_in, w_uq, w_ukv)
    cos_m, sin_m = _rope_tables(s, n_ctx, MLA_ROPE)
    tot = s + n_ctx
    cm = jnp.concatenate([jnp.ones((tot, MLA_NOPE), F32), cos_m, jnp.zeros((tot, LANES - 96), F32)], axis=1)
    sm = jnp.concatenate([jnp.zeros((tot, MLA_NOPE), F32), sin_m, jnp.zeros((tot, LANES - 96), F32)], axis=1)
    cos_g, sin_g = _rope_tables(s, n_ctx, GQA_HEAD_DIM)
    cg = jnp.concatenate([cos_g, cos_g], axis=1)
    sg = jnp.concatenate([sin_g, sin_g], axis=1)
    sh2 = jnp.stack([sh_c, sh], axis=0)
    sc2 = jnp.stack([sc_c, sc], axis=0)
    mq, mk, mv, gq, gk, gv = _attn_prep(x, ctx, g, sh2, sc2, w_aug, q_norm_g.reshape(1, -1),
                                        kv_norm_g.reshape(1, -1), wq, wkv, cm, sm, cg, sg)
    mla_out = _mla_attention(mq, mk, mv, tq=512, tk=1280 if tot % 1280 == 0 else n_ctx)
    gqa_out = _gqa_attention(sink, gq, gk, gv, n_ctx)
    wo = w_out.astype(BF16)
    nm = MLA_HEADS * MLA_V
    return _resid_proj2(mla_out, gqa_out, wo[:nm], wo[nm:], x, gate, tm=512)


def kernel(x, c, ctx, c_ctx, mod_w, mod_b, norm1_g, norm2_g, attn_w_in, mla_q_norm_g, mla_kv_norm_g,
           mla_w_uq, mla_w_ukv, gqa_sink, attn_w_out, hy_w_in, hy_b_in, hy_short_w, hy_short_b,
           hy_ffn_w1, hy_ffn_b1, hy_ffn_w2, hy_ffn_b2, hy_ffn_w3, hy_ffn_b3, hy_sin_freq, hy_ffn_w4,
           hy_filter_bias, hy_w_out, hy_b_out, moe_router_w, moe_router_b, moe_w1, moe_b1, moe_w2,
           moe_b2, final_norm_g):
    batch, seq, d = x.shape
    assert batch == 1
    depth = mod_w.shape[0]
    xs = x.reshape(seq, d)
    cs = ctx.reshape(-1, d)
    mod = _modulation(c, c_ctx, mod_w, mod_b)
    fg = final_norm_g.reshape(1, d)
    for i in range(depth):
        m = mod[i, 0].reshape(6, 1, d)
        mc = mod[i, 1].reshape(6, 1, d)
        g1 = norm1_g[i].reshape(1, d)
        j = i // 2
        if i % 2 == 0:
            xs = _attention_mixer(xs, cs, g1, m[0], m[1], mc[0], mc[1], m[2], attn_w_in[j], mla_q_norm_g[j],
                                  mla_kv_norm_g[j], mla_w_uq[j], mla_w_ukv[j], gqa_sink[j], attn_w_out[j])
        else:
            xs = _hyena_mixer(xs, g1, m[0], m[1], m[2], hy_w_in[j], hy_b_in[j], hy_short_w[j], hy_short_b[j],
                              hy_ffn_w1[j], hy_ffn_b1[j], hy_ffn_w2[j], hy_ffn_b2[j], hy_ffn_w3[j], hy_ffn_b3[j],
                              hy_sin_freq[j], hy_ffn_w4[j], hy_filter_bias[j], hy_w_out[j], hy_b_out[j])
        xs = _moe_layer(xs, norm2_g[i].reshape(1, d), m[3], m[4], m[5], moe_router_w[i], moe_router_b[i],
                        moe_w1[i], moe_b1[i], moe_w2[i], moe_b2[i], fg, final_norm=(i == depth - 1))
    return xs.reshape(batch, seq, d)
```

```python
import functools
import math

import jax
import jax.numpy as jnp
from jax import lax
from jax.experimental import pallas as pl
from jax.experimental.pallas import tpu as pltpu

F32 = jnp.float32
BF16 = jnp.bfloat16

RMS_EPS = 1e-6
ROPE_BASE = 10000.0
GRID_W = 64
NEG_INF = -1e30

MLA_HEADS = 8
MLA_Q_RANK = 256
MLA_KV_RANK = 128
MLA_NOPE = 64
MLA_ROPE = 32
MLA_V = 64
MLA_SCALE = 1.0 / math.sqrt(MLA_NOPE + MLA_ROPE)

GQA_Q_HEADS = 8
GQA_KV_HEADS = 2
GQA_HEAD_DIM = 64
GQA_SCALE = 1.0 / math.sqrt(GQA_HEAD_DIM)
WIN_BLOCK = 128

OFF_CKV = 0
OFF_KROPE = OFF_CKV + MLA_KV_RANK
OFF_GK = OFF_KROPE + MLA_ROPE
OFF_GV = OFF_GK + GQA_KV_HEADS * GQA_HEAD_DIM
KV_COLS = OFF_GV + GQA_KV_HEADS * GQA_HEAD_DIM
OFF_CQ = KV_COLS
OFF_GQ = OFF_CQ + MLA_Q_RANK
PROJ_COLS = OFF_GQ + GQA_Q_HEADS * GQA_HEAD_DIM

HY_BANDS = 16
HY_FILTER_FF = 64
HY_DECAY_TARGET = 1e-2
HY_FAST_DECAY = 0.3
HY_SLOW_DECAY = 1.5

N_EXPERTS = 32
TOP_K = 4
SWIGLU_LIMIT = 7.0
SWIGLU_ALPHA = 1.702

LANES = 128
VMEM_LIMIT_BYTES = 56 * 1024 * 1024

C_CKV = 0
C_CQ = C_CKV + 128
C_GQ = C_CQ + 256
C_GQR = C_GQ + 512
C_GK = C_GQR + 512
C_GKR = C_GK + 512
C_GV = C_GKR + 512
C_KR = C_GV + 256
C_KRR = C_KR + 128
AUG_COLS = C_KRR + 128


def _params(*sem):
    return pltpu.CompilerParams(dimension_semantics=sem, vmem_limit_bytes=VMEM_LIMIT_BYTES)


def _const_spec(shape):
    n = len(shape)
    return pl.BlockSpec(shape, lambda *_: (0,) * n)


def _norm_mod(x, g, sh, sc):
    ms = jnp.mean(x * x, axis=-1, keepdims=True)
    return (x * lax.rsqrt(ms + RMS_EPS) * g) * (1.0 + sc) + sh


def _rms(x, g):
    ms = jnp.mean(x * x, axis=-1, keepdims=True)
    return x * lax.rsqrt(ms + RMS_EPS) * g


def _mod_kernel(cc_ref, w_ref, b_ref, o_ref):
    cc = cc_ref[...]
    s = cc / (1.0 + jnp.exp(-cc))
    w = w_ref[...]
    b = b_ref[...]
    o_ref[0:1, :] = jnp.sum(w * s[:, 0:1], axis=0, keepdims=True) + b
    o_ref[1:2, :] = jnp.sum(w * s[:, 1:2], axis=0, keepdims=True) + b


def _modulation(c, c_ctx, mod_w, mod_b):
    depth, d, n = mod_w.shape
    tn = 1536
    cc = jnp.stack([c.reshape(d), c_ctx.reshape(d)], axis=1)
    return pl.pallas_call(
        _mod_kernel,
        grid=(depth, n // tn),
        in_specs=[
            _const_spec((d, 2)),
            pl.BlockSpec((None, d, tn), lambda l, j: (l, 0, j)),
            pl.BlockSpec((None, 1, tn), lambda l, j: (l, 0, j)),
        ],
        out_specs=pl.BlockSpec((None, 2, tn), lambda l, j: (l, 0, j)),
        out_shape=jax.ShapeDtypeStruct((depth, 2, n), F32),
        compiler_params=_params("parallel", "parallel"),
        name="modulation",
    )(cc, mod_w, mod_b.reshape(depth, 1, n))


def _rot_cols_1d(w):
    half = w.shape[-1] // 2
    return jnp.concatenate([-w[..., half:], w[..., :half]], axis=-1)


def _rot_cols_2d(w):
    half = w.shape[-1] // 2
    return jnp.concatenate([_rot_cols_1d(w[..., :half]), _rot_cols_1d(w[..., half:])], axis=-1)


def _rope_tables(n_lat, n_ctx, dims):
    t = jnp.arange(n_lat, dtype=jnp.int32)
    row = (t // GRID_W).astype(F32)
    col = (t % GRID_W).astype(F32)
    q = dims // 4
    inv = ROPE_BASE ** (-jnp.arange(q, dtype=F32) / q)
    ar = row[:, None] * inv[None, :]
    ac = col[:, None] * inv[None, :]
    cos = jnp.concatenate([jnp.cos(ar), jnp.cos(ar), jnp.cos(ac), jnp.cos(ac)], axis=-1)
    sin = jnp.concatenate([jnp.sin(ar), jnp.sin(ar), jnp.sin(ac), jnp.sin(ac)], axis=-1)
    cos = jnp.concatenate([jnp.ones((n_ctx, dims), F32), cos], axis=0)
    sin = jnp.concatenate([jnp.zeros((n_ctx, dims), F32), sin], axis=0)
    return cos, sin


def _attn_weights(w_in, w_uq, w_ukv):
    d = w_in.shape[0]
    z64 = jnp.zeros((d, 64), F32)
    z32 = jnp.zeros((d, 32), F32)
    ckv = w_in[:, OFF_CKV:OFF_KROPE]
    krope = w_in[:, OFF_KROPE:OFF_GK]
    gk = w_in[:, OFF_GK:OFF_GV]
    gv = w_in[:, OFF_GV:KV_COLS]
    cq = w_in[:, OFF_CQ:OFF_GQ]
    gq = w_in[:, OFF_GQ:PROJ_COLS]
    gq_rot = _rot_cols_2d(gq.reshape(d, GQA_Q_HEADS, GQA_HEAD_DIM)).reshape(d, -1)
    k0, k1 = gk[:, :64], gk[:, 64:]
    k0r, k1r = _rot_cols_2d(k0), _rot_cols_2d(k1)
    gk4 = jnp.concatenate([k0, z64, z64, k0, k1, z64, z64, k1], axis=1)
    gk4r = jnp.concatenate([k0r, z64, z64, k0r, k1r, z64, z64, k1r], axis=1)
    v0, v1 = gv[:, :64], gv[:, 64:]
    gv2 = jnp.concatenate([v0, v0, v1, v1], axis=1)
    kr = jnp.concatenate([z64, krope, z32], axis=1)
    krr = jnp.concatenate([z64, _rot_cols_2d(krope), z32], axis=1)
    w_aug = jnp.concatenate([ckv, cq, gq, gq_rot, gk4, gk4r, gv2, kr, krr], axis=1).astype(BF16)

    r = w_uq.shape[0]
    q3 = w_uq.reshape(r, MLA_HEADS, MLA_NOPE + MLA_ROPE)
    q_main = jnp.pad(q3, ((0, 0), (0, 0), (0, LANES - MLA_NOPE - MLA_ROPE)))
    q_rot = jnp.concatenate([jnp.zeros((r, MLA_HEADS, MLA_NOPE), F32), _rot_cols_2d(q3[..., MLA_NOPE:]),
                             jnp.zeros((r, MLA_HEADS, LANES - MLA_NOPE - MLA_ROPE), F32)], axis=-1)
    wq = jnp.concatenate([q_main.reshape(r, -1), q_rot.reshape(r, -1)], axis=1).astype(BF16)

    rk = w_ukv.shape[0]
    kv3 = w_ukv.reshape(rk, MLA_HEADS, MLA_NOPE + MLA_V)
    k_part = jnp.pad(kv3[..., :MLA_NOPE], ((0, 0), (0, 0), (0, LANES - MLA_NOPE))).reshape(rk, -1)
    v_part = kv3[..., MLA_NOPE:].reshape(rk, -1)
    wkv = jnp.concatenate([k_part, v_part], axis=1).astype(BF16)
    return w_aug, wq, wkv


def _attn_prep_kernel(x_ref, ctx_ref, g_ref, sh_ref, sc_ref, w_ref, gq_ref, gkv_ref, wq_ref, wkv_ref,
                      cm_ref, sm_ref, cg_ref, sg_ref,
                      mq_ref, mk_ref, mv_ref, gqo_ref, gko_ref, gvo_ref):
    i = pl.program_id(0)
    xin = jnp.where(i == 0, ctx_ref[...], x_ref[...])
    h = _norm_mod(xin, g_ref[...], sh_ref[...], sc_ref[...])
    p = jnp.dot(h.astype(BF16), w_ref[...], preferred_element_type=F32)

    ckv = _rms(p[:, C_CKV:C_CKV + 128], gkv_ref[...])
    kv = jnp.dot(ckv.astype(BF16), wkv_ref[...], preferred_element_type=F32)
    cq = _rms(p[:, C_CQ:C_CQ + 256], gq_ref[...])
    qa = jnp.dot(cq.astype(BF16), wq_ref[...], preferred_element_type=F32)

    cm, sm = cm_ref[...], sm_ref[...]
    kr = p[:, C_KR:C_KR + 128] * cm + p[:, C_KRR:C_KRR + 128] * sm
    for hd in range(MLA_HEADS):
        lo = hd * LANES
        qh = qa[:, lo:lo + LANES] * cm + qa[:, 1024 + lo:1024 + lo + LANES] * sm
        mq_ref[hd] = (qh * MLA_SCALE).astype(BF16)
        mk_ref[hd] = (kv[:, lo:lo + LANES] + kr).astype(BF16)
    mv_ref[...] = kv[:, 1024:1536].astype(BF16)

    cg, sg = cg_ref[...], sg_ref[...]
    for gI in range(4):
        lo = gI * LANES
        gq = p[:, C_GQ + lo:C_GQ + lo + LANES] * cg + p[:, C_GQR + lo:C_GQR + lo + LANES] * sg
        gqo_ref[:, lo:lo + LANES] = (gq * GQA_SCALE).astype(BF16)
        gk = p[:, C_GK + lo:C_GK + lo + LANES] * cg + p[:, C_GKR + lo:C_GKR + lo + LANES] * sg
        gko_ref[:, lo:lo + LANES] = gk.astype(BF16)
    gvo_ref[...] = p[:, C_GV:C_GV + 256].astype(BF16)


def _attn_prep(x, ctx, g, sh2, sc2, w_aug, gq, gkv, wq, wkv, cm, sm, cg, sg):
    s, d = x.shape
    c = ctx.shape[0]
    tm = c
    nt = s // tm + 1
    tot = s + c
    lat = lambda i: (jnp.maximum(i - 1, 0), 0)
    row = lambda i: (i, 0)
    return pl.pallas_call(
        _attn_prep_kernel,
        grid=(nt,),
        in_specs=[
            pl.BlockSpec((tm, d), lat),
            _const_spec((c, d)),
            _const_spec((1, d)),
            pl.BlockSpec((None, 1, d), lambda i: (jnp.minimum(i, 1), 0, 0)),
            pl.BlockSpec((None, 1, d), lambda i: (jnp.minimum(i, 1), 0, 0)),
            _const_spec(w_aug.shape),
            _const_spec(gq.shape),
            _const_spec(gkv.shape),
            _const_spec(wq.shape),
            _const_spec(wkv.shape),
            pl.BlockSpec((tm, LANES), row),
            pl.BlockSpec((tm, LANES), row),
            pl.BlockSpec((tm, LANES), row),
            pl.BlockSpec((tm, LANES), row),
        ],
        out_specs=[
            pl.BlockSpec((MLA_HEADS, tm, LANES), lambda i: (0, jnp.maximum(i - 1, 0), 0)),
            pl.BlockSpec((MLA_HEADS, tm, LANES), lambda i: (0, i, 0)),
            pl.BlockSpec((tm, 512), row),
            pl.BlockSpec((tm, 512), lat),
            pl.BlockSpec((tm, 512), row),
            pl.BlockSpec((tm, 256), row),
        ],
        out_shape=[
            jax.ShapeDtypeStruct((MLA_HEADS, s, LANES), BF16),
            jax.ShapeDtypeStruct((MLA_HEADS, tot, LANES), BF16),
            jax.ShapeDtypeStruct((tot, 512), BF16),
            jax.ShapeDtypeStruct((s, 512), BF16),
            jax.ShapeDtypeStruct((tot, 512), BF16),
            jax.ShapeDtypeStruct((tot, 256), BF16),
        ],
        compiler_params=_params("arbitrary"),
        name="attn_prep",
    )(x, ctx, g, sh2, sc2, w_aug, gq, gkv, wq, wkv, cm, sm, cg, sg)


def _mla_kernel(q_ref, k_ref, v_ref, o_ref, m_sc, l_sc, acc_sc):
    j = pl.program_id(2)

    @pl.when(j == 0)
    def _():
        m_sc[...] = jnp.full(m_sc.shape, -jnp.inf, F32)
        l_sc[...] = jnp.zeros(l_sc.shape, F32)
        acc_sc[...] = jnp.zeros(acc_sc.shape, F32)

    v = v_ref[...]
    for hh in range(2):
        s = lax.dot_general(q_ref[hh], k_ref[hh], (((1,), (1,)), ((), ())),
                            preferred_element_type=F32)
        m_prev = m_sc[hh]
        m_new = jnp.maximum(m_prev, jnp.max(s, axis=-1, keepdims=True))
        alpha = jnp.exp(m_prev - m_new)
        p = jnp.exp(s - m_new)
        l_sc[hh] = alpha * l_sc[hh] + jnp.sum(p, axis=-1, keepdims=True)
        acc_sc[hh] = alpha * acc_sc[hh] + jnp.dot(p.astype(BF16), v, preferred_element_type=F32)
        m_sc[hh] = m_new

    @pl.when(j == pl.num_programs(2) - 1)
    def _():
        lane = lax.broadcasted_iota(jnp.int32, acc_sc.shape[1:], 1)
        o0 = acc_sc[0] / l_sc[0]
        o1 = acc_sc[1] / l_sc[1]
        o_ref[...] = jnp.where(lane < MLA_V, o0, o1).astype(o_ref.dtype)


def _mla_attention(mq, mk, mv, tq, tk):
    _, s, _ = mq.shape
    tot = mk.shape[1]
    return pl.pallas_call(
        _mla_kernel,
        grid=(MLA_HEADS // 2, s // tq, tot // tk),
        in_specs=[
            pl.BlockSpec((2, tq, LANES), lambda h, i, j: (h, i, 0)),
            pl.BlockSpec((2, tk, LANES), lambda h, i, j: (h, j, 0)),
            pl.BlockSpec((tk, LANES), lambda h, i, j: (j, h)),
        ],
        out_specs=pl.BlockSpec((tq, LANES), lambda h, i, j: (i, h)),
        out_shape=jax.ShapeDtypeStruct((s, MLA_HEADS * MLA_V), BF16),
        scratch_shapes=[
            pltpu.VMEM((2, tq, 1), F32),
            pltpu.VMEM((2, tq, 1), F32),
            pltpu.VMEM((2, tq, LANES), F32),
        ],
        compiler_params=_params("parallel", "parallel", "arbitrary"),
        name="mla_flash",
    )(mq, mk, mv)


def _gqa_kernel(sink_ref, q_ref, kp_ref, ko_ref, kn_ref, kc_ref, vp_ref, vo_ref, vn_ref, vc_ref, o_ref):
    n = pl.program_id(0)
    nb = pl.num_programs(0)
    wb = WIN_BLOCK
    r = lax.broadcasted_iota(jnp.int32, (wb, wb), 0)
    c = lax.broadcasted_iota(jnp.int32, (wb, wb), 1)
    ok_prev = c >= r + jnp.where(n >= 1, 0, wb)
    ok_next = c + jnp.where(n <= nb - 2, 0, wb) <= r
    lane = lax.broadcasted_iota(jnp.int32, (wb, LANES), 1)
    dn = (((1,), (1,)), ((), ()))
    for pr in range(GQA_Q_HEADS // 2):
        kvh = pr // 2
        q = q_ref[:, pr * LANES:(pr + 1) * LANES]
        vlo = kvh * LANES
        outs = []
        for var in range(2):
            hd = 2 * pr + var
            klo = (2 * kvh + var) * LANES
            s_p = lax.dot_general(q, kp_ref[:, klo:klo + LANES], dn, preferred_element_type=F32)
            s_o = lax.dot_general(q, ko_ref[:, klo:klo + LANES], dn, preferred_element_type=F32)
            s_n = lax.dot_general(q, kn_ref[:, klo:klo + LANES], dn, preferred_element_type=F32)
            s_c = lax.dot_general(q, kc_ref[:, klo:klo + LANES], dn, preferred_element_type=F32)
            s_p = jnp.where(ok_prev, s_p, NEG_INF)
            s_n = jnp.where(ok_next, s_n, NEG_INF)
            sink = sink_ref[hd]
            m = jnp.maximum(jnp.maximum(jnp.max(s_p, axis=-1, keepdims=True),
                                        jnp.max(s_o, axis=-1, keepdims=True)),
                            jnp.maximum(jnp.max(s_n, axis=-1, keepdims=True),
                                        jnp.max(s_c, axis=-1, keepdims=True)))
            m = jnp.maximum(m, sink)
            e_p, e_o, e_n, e_c = jnp.exp(s_p - m), jnp.exp(s_o - m), jnp.exp(s_n - m), jnp.exp(s_c - m)
            den = (jnp.sum(e_p, axis=-1, keepdims=True) + jnp.sum(e_o, axis=-1, keepdims=True)
                   + jnp.sum(e_n, axis=-1, keepdims=True) + jnp.sum(e_c, axis=-1, keepdims=True)
                   + jnp.exp(sink - m))
            o = (jnp.dot(e_p.astype(BF16), vp_ref[:, vlo:vlo + LANES], preferred_element_type=F32)
                 + jnp.dot(e_o.astype(BF16), vo_ref[:, vlo:vlo + LANES], preferred_element_type=F32)
                 + jnp.dot(e_n.astype(BF16), vn_ref[:, vlo:vlo + LANES], preferred_element_type=F32)
                 + jnp.dot(e_c.astype(BF16), vc_ref[:, vlo:vlo + LANES], preferred_element_type=F32))
            outs.append(o / den)
        o_ref[:, pr * LANES:(pr + 1) * LANES] = jnp.where(lane < GQA_HEAD_DIM, outs[0], outs[1]).astype(o_ref.dtype)


def _gqa_attention(sink, gq, gk, gv, n_ctx):
    s = gq.shape[0]
    wb = WIN_BLOCK
    nb = s // wb
    cb = n_ctx // wb
    prev = lambda n: (n + cb - 1, 0)
    own = lambda n: (n + cb, 0)
    nxt = lambda n: (jnp.minimum(n + cb + 1, nb + cb - 1), 0)
    return pl.pallas_call(
        _gqa_kernel,
        grid=(nb,),
        in_specs=[
            pl.BlockSpec(memory_space=pltpu.SMEM),
            pl.BlockSpec((wb, 512), lambda n: (n, 0)),
            pl.BlockSpec((wb, 512), prev),
            pl.BlockSpec((wb, 512), own),
            pl.BlockSpec((wb, 512), nxt),
            _const_spec((n_ctx, 512)),
            pl.BlockSpec((wb, 256), prev),
            pl.BlockSpec((wb, 256), own),
            pl.BlockSpec((wb, 256), nxt),
            _const_spec((n_ctx, 256)),
        ],
        out_specs=pl.BlockSpec((wb, 512), lambda n: (n, 0)),
        out_shape=jax.ShapeDtypeStruct((s, 512), BF16),
        compiler_params=_params("parallel"),
        name="gqa_window",
    )(sink, gq, gk, gk, gk, gk, gv, gv, gv, gv)


def _resid_proj2_kernel(a_ref, b_ref, wa_ref, wb_ref, x_ref, gt_ref, o_ref):
    y = (jnp.dot(a_ref[...], wa_ref[...], preferred_element_type=F32)
         + jnp.dot(b_ref[...], wb_ref[...], preferred_element_type=F32))
    o_ref[...] = x_ref[...] + gt_ref[...] * y


def _resid_proj2(a, b, wa, wb, x, gate, tm):
    s, d = x.shape
    row = lambda i: (i, 0)
    return pl.pallas_call(
        _resid_proj2_kernel,
        grid=(s // tm,),
        in_specs=[pl.BlockSpec((tm, a.shape[1]), row), pl.BlockSpec((tm, b.shape[1]), row),
                  _const_spec(wa.shape), _const_spec(wb.shape),
                  pl.BlockSpec((tm, d), row), _const_spec((1, d))],
        out_specs=pl.BlockSpec((tm, d), row),
        out_shape=jax.ShapeDtypeStruct((s, d), F32),
        compiler_params=_params("parallel"),
        name="attn_out_proj",
    )(a, b, wa, wb, x, gate)


def _resid_proj1_kernel(a_ref, w_ref, bias_ref, x_ref, gt_ref, o_ref):
    y = jnp.dot(a_ref[...], w_ref[...], preferred_element_type=F32) + bias_ref[...]
    o_ref[...] = x_ref[...] + gt_ref[...] * y


def _resid_proj1(a, w, bias, x, gate, tm):
    s, d = x.shape
    row = lambda i: (i, 0)
    return pl.pallas_call(
        _resid_proj1_kernel,
        grid=(s // tm,),
        in_specs=[pl.BlockSpec((tm, a.shape[1]), row), _const_spec(w.shape), _const_spec((1, d)),
                  pl.BlockSpec((tm, d), row), _const_spec((1, d))],
        out_specs=pl.BlockSpec((tm, d), row),
        out_shape=jax.ShapeDtypeStruct((s, d), F32),
        compiler_params=_params("parallel"),
        name="hyena_out_proj",
    )(a, w, bias, x, gate)


META_IDX, META_W, META_RANK = 0, 4, 8


def _router_kernel(x_ref, g_ref, sh_ref, sc_ref, rw_ref, rb_ref, t_ref, meta_ref, cnt_ref, carry_sc):
    i = pl.program_id(0)

    @pl.when(i == 0)
    def _():
        carry_sc[...] = jnp.zeros(carry_sc.shape, F32)

    t = _norm_mod(x_ref[...], g_ref[...], sh_ref[...], sc_ref[...])
    t_ref[...] = t
    logits = jnp.dot(t, rw_ref[...], preferred_element_type=F32,
                     precision=lax.Precision.HIGHEST) + rb_ref[...]
    tm = logits.shape[0]
    lane = lax.broadcasted_iota(jnp.int32, logits.shape, 1)
    work = logits
    vals, hots, idxs = [], [], []
    for _ in range(TOP_K):
        m = jnp.max(work, axis=-1, keepdims=True)
        idx = jnp.min(jnp.where(work == m, lane, N_EXPERTS), axis=-1, keepdims=True)
        hot = lane == idx
        vals.append(m)
        hots.append(hot)
        idxs.append(idx)
        work = jnp.where(hot, -jnp.inf, work)
    es = [jnp.exp(v - vals[0]) for v in vals]
    den = es[0] + es[1] + es[2] + es[3]
    hot_all = jnp.zeros(logits.shape, F32)
    for hot in hots:
        hot_all = hot_all + jnp.where(hot, 1.0, 0.0)
    r = lax.broadcasted_iota(jnp.int32, (tm, tm), 0)
    c = lax.broadcasted_iota(jnp.int32, (tm, tm), 1)
    tri = jnp.where(r > c, 1.0, 0.0).astype(BF16)
    rank = jnp.dot(tri, hot_all.astype(BF16), preferred_element_type=F32) + carry_sc[...]
    carry_sc[...] += jnp.sum(hot_all, axis=0, keepdims=True)
    cnt_ref[...] = carry_sc[...]
    lane_m = lax.broadcasted_iota(jnp.int32, (tm, LANES), 1)
    meta = jnp.zeros((tm, LANES), F32)
    for k in range(TOP_K):
        rk = jnp.sum(jnp.where(hots[k], rank, 0.0), axis=-1, keepdims=True)
        meta = jnp.where(lane_m == META_IDX + k, idxs[k].astype(F32), meta)
        meta = jnp.where(lane_m == META_W + k, es[k] / den, meta)
        meta = jnp.where(lane_m == META_RANK + k, rk, meta)
    meta_ref[...] = meta


def _router(x, g, sh, sc, rw, rb, tm):
    s, d = x.shape
    e = rw.shape[1]
    row = lambda i: (i, 0)
    return pl.pallas_call(
        _router_kernel,
        grid=(s // tm,),
        in_specs=[pl.BlockSpec((tm, d), row), _const_spec((1, d)), _const_spec((1, d)), _const_spec((1, d)),
                  _const_spec((d, e)), _const_spec((1, e))],
        out_specs=[pl.BlockSpec((tm, d), row), pl.BlockSpec((tm, LANES), row), _const_spec((1, e))],
        out_shape=[jax.ShapeDtypeStruct((s, d), F32), jax.ShapeDtypeStruct((s, LANES), F32),
                   jax.ShapeDtypeStruct((1, e), F32)],
        scratch_shapes=[pltpu.VMEM((1, e), F32)],
        compiler_params=_params("arbitrary"),
        name="moe_router",
    )(x, g, sh, sc, rw, rb)


def _rows_copy(src, dst, sem, n):
    return pltpu.make_async_copy(src.at[pl.ds(0, n)], dst.at[pl.ds(0, n)], sem)


def _dispatch_kernel(pos_ref, t_ref, xs_init_ref, xs_ref, sem, *, tm):
    del xs_init_ref
    base = pl.program_id(0) * tm

    def body(r, carry):
        for k in range(TOP_K):
            p = pos_ref[(base + r) * TOP_K + k]
            pltpu.make_async_copy(t_ref.at[pl.ds(r, 1)], xs_ref.at[pl.ds(p, 1)], sem).start()
        return carry

    lax.fori_loop(0, tm, body, 0, unroll=8)
    for k in range(TOP_K):
        _rows_copy(t_ref, xs_ref, sem, tm).wait()


def _dispatch(pos, t, rows, tm):
    s, d = t.shape
    grid_spec = pltpu.PrefetchScalarGridSpec(
        num_scalar_prefetch=1,
        grid=(s // tm,),
        in_specs=[pl.BlockSpec((tm, d), lambda i, pos: (i, 0)),
                  pl.BlockSpec(memory_space=pl.ANY)],
        out_specs=pl.BlockSpec(memory_space=pl.ANY),
        scratch_shapes=[pltpu.SemaphoreType.DMA(())],
    )
    return pl.pallas_call(
        functools.partial(_dispatch_kernel, tm=tm),
        grid_spec=grid_spec,
        out_shape=jax.ShapeDtypeStruct((rows, d), F32),
        input_output_aliases={2: 0},
        compiler_params=_params("arbitrary"),
        name="moe_dispatch",
    )(pos, t, jnp.zeros((rows, d), F32))


def _moe_expert_kernel(te_ref, nu_ref, xs_ref, w1_ref, b1_ref, w2_ref, b2_ref, y_ref, w1b, w2b, *, ff):
    i = pl.program_id(0)
    e = te_ref[i]
    e_prev = te_ref[jnp.maximum(i - 1, 0)]

    @pl.when((i == 0) | (e != e_prev))
    def _():
        w1b[...] = w1_ref[...].astype(BF16)
        w2b[...] = w2_ref[...].astype(BF16)

    @pl.when(i < nu_ref[0])
    def _():
        u = jnp.dot(xs_ref[...].astype(BF16), w1b[...], preferred_element_type=F32) + b1_ref[...]
        glu = jnp.minimum(u[:, :ff], SWIGLU_LIMIT)
        lin = jnp.clip(u[:, ff:], -SWIGLU_LIMIT, SWIGLU_LIMIT)
        act = glu / (1.0 + jnp.exp(-SWIGLU_ALPHA * glu)) * (lin + 1.0)
        y_ref[...] = jnp.dot(act.astype(BF16), w2b[...], preferred_element_type=F32) + b2_ref[...]

    @pl.when(i >= nu_ref[0])
    def _():
        y_ref[...] = jnp.zeros(y_ref.shape, F32)


def _moe_experts(tile_expert, n_used, xs, w1, b1, w2, b2, tmm):
    rows, d = xs.shape
    ne, _, ff2 = w1.shape
    ff = ff2 // 2
    wsel = lambda i, te, nu: (te[i], 0, 0)
    grid_spec = pltpu.PrefetchScalarGridSpec(
        num_scalar_prefetch=2,
        grid=(rows // tmm,),
        in_specs=[pl.BlockSpec((tmm, d), lambda i, te, nu: (jnp.minimum(i, nu[0] - 1), 0)),
                  pl.BlockSpec((None, d, ff2), wsel),
                  pl.BlockSpec((None, 1, ff2), wsel),
                  pl.BlockSpec((None, ff, d), wsel),
                  pl.BlockSpec((None, 1, d), wsel)],
        out_specs=pl.BlockSpec((tmm, d), lambda i, te, nu: (i, 0)),
        scratch_shapes=[pltpu.VMEM((d, ff2), BF16), pltpu.VMEM((ff, d), BF16)],
    )
    return pl.pallas_call(
        functools.partial(_moe_expert_kernel, ff=ff),
        grid_spec=grid_spec,
        out_shape=jax.ShapeDtypeStruct((rows, d), F32),
        compiler_params=_params("arbitrary"),
        name="moe_experts",
    )(tile_expert, n_used, xs, w1, b1.reshape(ne, 1, ff2), w2, b2.reshape(ne, 1, d))


def _combine_kernel(pos_ref, y_ref, meta_ref, x_ref, gt_ref, fg_ref, o_ref, buf, sem, *, tc, final_norm):
    base = pl.program_id(0) * tc

    def body(r, carry):
        for k in range(TOP_K):
            p = pos_ref[(base + r) * TOP_K + k]
            pltpu.make_async_copy(y_ref.at[pl.ds(p, 1)], buf.at[k, pl.ds(r, 1)], sem).start()
        return carry

    lax.fori_loop(0, tc, body, 0, unroll=8)
    for k in range(TOP_K):
        _rows_copy(y_ref, buf.at[k], sem, tc).wait()
    meta = meta_ref[...]
    acc = buf[0] * meta[:, META_W:META_W + 1]
    for k in range(1, TOP_K):
        acc = acc + buf[k] * meta[:, META_W + k:META_W + k + 1]
    xo = x_ref[...] + gt_ref[...] * acc
    if final_norm:
        xo = _rms(xo, fg_ref[...])
    o_ref[...] = xo


def _combine(pos, y, meta, x, gate, final_g, tc, final_norm):
    s, d = x.shape
    row = lambda i, pos: (i, 0)
    grid_spec = pltpu.PrefetchScalarGridSpec(
        num_scalar_prefetch=1,
        grid=(s // tc,),
        in_specs=[pl.BlockSpec(memory_space=pl.ANY),
                  pl.BlockSpec((tc, LANES), row),
                  pl.BlockSpec((tc, d), row),
                  pl.BlockSpec((1, d), lambda i, pos: (0, 0)),
                  pl.BlockSpec((1, d), lambda i, pos: (0, 0))],
        out_specs=pl.BlockSpec((tc, d), row),
        scratch_shapes=[pltpu.VMEM((TOP_K, tc, d), F32), pltpu.SemaphoreType.DMA(())],
    )
    return pl.pallas_call(
        functools.partial(_combine_kernel, tc=tc, final_norm=final_norm),
        grid_spec=grid_spec,
        out_shape=jax.ShapeDtypeStruct((s, d), F32),
        compiler_params=_params("arbitrary"),
        name="moe_combine",
    )(pos, y, meta, x, gate, final_g)


MOE_ROW_TILE = 256


def _moe_layer(x, g, sh, sc, gate, rw, rb, w1, b1, w2, b2, final_g, final_norm):
    s, d = x.shape
    ne = rw.shape[1]
    tmm = MOE_ROW_TILE
    t, meta, cnt = _router(x, g, sh, sc, rw, rb.reshape(1, -1), tm=512)
    idx = meta[:, META_IDX:META_IDX + TOP_K].astype(jnp.int32)
    rank = meta[:, META_RANK:META_RANK + TOP_K].astype(jnp.int32)
    counts = cnt[0].astype(jnp.int32)
    padded = ((counts + tmm - 1) // tmm) * tmm
    ends = jnp.cumsum(padded)
    pos = (jnp.take(ends - padded, idx) + rank).reshape(-1)
    rows = TOP_K * s + ne * tmm
    nt = rows // tmm
    tile_ends = ends // tmm
    n_used = tile_ends[-1]
    tiles = jnp.arange(nt, dtype=jnp.int32)
    te = jnp.searchsorted(tile_ends, tiles, side='right').astype(jnp.int32)
    te = jnp.where(tiles < n_used, te, te[jnp.maximum(n_used - 1, 0)])
    te = jnp.minimum(te, ne - 1)
    xs = _dispatch(pos, t, rows, tm=512)
    y = _moe_experts(te, n_used.reshape(1), xs, w1, b1, w2, b2, tmm)
    return _combine(pos, y, meta, x, gate, final_g, tc=256, final_norm=final_norm)


def _hy_in_kernel(xp_ref, x_ref, xn_ref, g_ref, sh_ref, sc_ref, w_ref, b_ref, sw_ref, sb_ref,
                  v_ref, g1_ref, g2_ref, *, tm, seq):
    i = pl.program_id(0)
    d = x_ref.shape[1]
    xe = jnp.concatenate([xp_ref[...], x_ref[...], xn_ref[...]], axis=0)
    h = _norm_mod(xe, g_ref[...], sh_ref[...], sc_ref[...])
    p = jnp.dot(h.astype(BF16), w_ref[...], preferred_element_type=F32) + b_ref[...]
    row = i * tm - 8 + lax.broadcasted_iota(jnp.int32, (tm + 16, 1), 0)
    p = jnp.where((row >= 0) & (row < seq), p, 0.0)
    pm = pltpu.roll(p, 1, 0)[8:8 + tm]
    pc = p[8:8 + tm]
    pn = pltpu.roll(p, tm + 15, 0)[8:8 + tm]
    sw = sw_ref[...]
    out = sw[0:1] * pm + sw[1:2] * pc + sw[2:3] * pn + sb_ref[...]
    v_ref[...] = out[:, :d].astype(BF16)
    g1_ref[...] = out[:, d:2 * d].astype(BF16)
    g2_ref[...] = out[:, 2 * d:].astype(BF16)


def _hy_in(x, g, sh, sc, w, b, sw, sb, tm):
    s, d = x.shape
    n3 = w.shape[1]
    r8 = tm // 8
    nblk8 = s // 8
    row = lambda i: (i, 0)
    out = jax.ShapeDtypeStruct((s, d), BF16)
    return pl.pallas_call(
        functools.partial(_hy_in_kernel, tm=tm, seq=s),
        grid=(s // tm,),
        in_specs=[
            pl.BlockSpec((8, d), lambda i: (jnp.maximum(i * r8 - 1, 0), 0)),
            pl.BlockSpec((tm, d), row),
            pl.BlockSpec((8, d), lambda i: (jnp.minimum((i + 1) * r8, nblk8 - 1), 0)),
            _const_spec((1, d)), _const_spec((1, d)), _const_spec((1, d)),
            _const_spec((d, n3)), _const_spec((1, n3)), _const_spec((3, n3)), _const_spec((1, n3)),
        ],
        out_specs=[pl.BlockSpec((tm, d), row)] * 3,
        out_shape=[out, out, out],
        compiler_params=_params("parallel"),
        name="hyena_in_proj",
    )(x, x, x, g, sh, sc, w, b, sw, sb)


def _hy_filter_kernel(z_ref, w1_ref, b1_ref, w2_ref, b2_ref, w3_ref, b3_ref, fr_ref, w4_ref, dl_ref,
                      hf_ref, nrm_ref):
    i = pl.program_id(0)
    hp = lax.Precision.HIGHEST
    z = z_ref[...]
    fr = fr_ref[...]
    a = jnp.sin(fr * (jnp.dot(z, w1_ref[...], preferred_element_type=F32, precision=hp) + b1_ref[...]))
    a = jnp.sin(fr * (jnp.dot(a, w2_ref[...], preferred_element_type=F32, precision=hp) + b2_ref[...]))
    a = jnp.sin(fr * (jnp.dot(a, w3_ref[...], preferred_element_type=F32, precision=hp) + b3_ref[...]))
    hf = jnp.dot(a.astype(BF16), w4_ref[...], preferred_element_type=F32)
    decay = jnp.exp(-z[:, 0:1] * jnp.abs(dl_ref[...]))
    d = decay.shape[1]
    cols = []
    for q in range(4):
        cols.append(hf[:, q * d:(q + 1) * d] * decay)
    hf = jnp.concatenate(cols, axis=1)

    @pl.when(i == 0)
    def _():
        nrm_ref[...] = jnp.zeros(nrm_ref.shape, F32)

    nrm_ref[...] += jnp.sum(jnp.abs(hf), axis=0, keepdims=True)
    t0 = (i == 0) & (lax.broadcasted_iota(jnp.int32, hf.shape, 0) == 0)
    col = lax.broadcasted_iota(jnp.int32, hf.shape, 1)
    bwd = ((col >= d) & (col < 2 * d)) | (col >= 3 * d)
    hf_ref[...] = jnp.where(t0 & bwd, 0.0, hf).astype(BF16)


def _hy_filter(z, w1, b1, w2, b2, w3, b3, fr, w4, deltas, tl):
    seq = z.shape[0]
    n4 = w4.shape[1]
    ff = w2.shape[0]
    return pl.pallas_call(
        _hy_filter_kernel,
        grid=(seq // tl,),
        in_specs=[pl.BlockSpec((tl, z.shape[1]), lambda i: (i, 0)),
                  _const_spec(w1.shape), _const_spec((1, ff)), _const_spec(w2.shape), _const_spec((1, ff)),
                  _const_spec(w3.shape), _const_spec((1, ff)), _const_spec((1, ff)), _const_spec(w4.shape),
                  _const_spec(deltas.shape)],
        out_specs=[pl.BlockSpec((tl, n4), lambda i: (i, 0)), _const_spec((1, n4))],
        out_shape=[jax.ShapeDtypeStruct((seq, n4), BF16), jax.ShapeDtypeStruct((1, n4), F32)],
        compiler_params=_params("arbitrary"),
        name="hyena_filter_mlp",
    )(z, w1, b1, w2, b2, w3, b3, fr, w4, deltas)


def _dft_a_kernel(f_ref, u_ref, o_ref):
    o_ref[...] = jnp.dot(f_ref[...], u_ref[...], preferred_element_type=F32).astype(o_ref.dtype)


def _dft_a(f1, u2, tn):
    m, k = f1.shape
    cols = u2.shape[1]
    return pl.pallas_call(
        _dft_a_kernel,
        grid=(cols // tn,),
        in_specs=[_const_spec((m, k)), pl.BlockSpec((k, tn), lambda j: (0, j))],
        out_specs=pl.BlockSpec((m, tn), lambda j: (0, j)),
        out_shape=jax.ShapeDtypeStruct((m, cols), BF16),
        compiler_params=_params("parallel"),
        name="dft_stage_a",
    )(f1, u2)


def _spectrum_kernel(wf_ref, a_ref, inv_ref, k_ref, *, kb, n2):
    inv = inv_ref[...]
    d = inv.shape[1]
    for b in range(kb):
        a = a_ref[:, b].reshape(2 * n2, 2 * d)
        x = jnp.dot(wf_ref[b], a, preferred_element_type=F32)
        xr, xi = x[:n2], x[n2:]
        k_ref[0, b] = (xr[:, :d] + xr[:, d:]) * inv
        k_ref[1, b] = (xi[:, :d] - xi[:, d:]) * inv


def _filter_spectrum(wf, a4, inv_norm, kb):
    _, kh, n2, c4 = a4.shape
    d = c4 // 4
    return pl.pallas_call(
        functools.partial(_spectrum_kernel, kb=kb, n2=n2),
        grid=(2, kh // kb),
        in_specs=[pl.BlockSpec((kb, 2 * n2, 2 * n2), lambda o, j: (j, 0, 0)),
                  pl.BlockSpec((2, kb, n2, 2 * d), lambda o, j: (0, j, 0, o)),
                  pl.BlockSpec((None, 1, d), lambda o, j: (o, 0, 0))],
        out_specs=pl.BlockSpec((None, 2, kb, n2, d), lambda o, j: (o, 0, j, 0, 0)),
        out_shape=jax.ShapeDtypeStruct((2, 2, kh, n2, d), F32),
        compiler_params=_params("parallel", "parallel"),
        name="filter_spectrum",
    )(wf, a4, inv_norm)


def _dft_mid_kernel(wf_ref, wi_ref, a_ref, k_ref, o_ref, *, kb, n2):
    c = a_ref.shape[-1]
    for b in range(kb):
        a = a_ref[:, b].reshape(2 * n2, c)
        x = jnp.dot(wf_ref[b], a, preferred_element_type=F32)
        xr, xi = x[:n2], x[n2:]
        kr, ki = k_ref[0, b], k_ref[1, b]
        y = jnp.concatenate([xr * kr - xi * ki, xr * ki + xi * kr], axis=0).astype(BF16)
        bb = jnp.dot(wi_ref[b], y, preferred_element_type=F32)
        o_ref[:, b] = bb.reshape(2, n2, c).astype(o_ref.dtype)


def _dft_mid(wf, wi, a4, kspec, order, kb):
    _, kh, n2, c = a4.shape
    return pl.pallas_call(
        functools.partial(_dft_mid_kernel, kb=kb, n2=n2),
        grid=(kh // kb,),
        in_specs=[pl.BlockSpec((kb, 2 * n2, 2 * n2), lambda j: (j, 0, 0)),
                  pl.BlockSpec((kb, 2 * n2, 2 * n2), lambda j: (j, 0, 0)),
                  pl.BlockSpec((2, kb, n2, c), lambda j: (0, j, 0, 0)),
                  pl.BlockSpec((None, 2, kb, n2, c), lambda j: (order, 0, j, 0, 0))],
        out_specs=pl.BlockSpec((2, kb, n2, c), lambda j: (0, j, 0, 0)),
        out_shape=jax.ShapeDtypeStruct(a4.shape, BF16),
        compiler_params=_params("parallel"),
        name="dft_middle",
    )(wf, wi, a4, kspec)


def _dft_c_kernel(f_ref, b_ref, u_ref, g_ref, bias_ref, o_ref):
    y = jnp.dot(f_ref[...], b_ref[...], preferred_element_type=F32)
    u = u_ref[...].astype(F32)
    o_ref[...] = (g_ref[...].astype(F32) * (y + u * bias_ref[...])).astype(o_ref.dtype)


def _dft_c(f3, b2, u2, g2, bias_t, tn):
    m, k = f3.shape
    cols = b2.shape[1]
    col = lambda j: (0, j)
    return pl.pallas_call(
        _dft_c_kernel,
        grid=(cols // tn,),
        in_specs=[_const_spec((m, k)), pl.BlockSpec((k, tn), col), pl.BlockSpec((m, tn), col),
                  pl.BlockSpec((m, tn), col), _const_spec((1, tn))],
        out_specs=pl.BlockSpec((m, tn), col),
        out_shape=jax.ShapeDtypeStruct((m, cols), BF16),
        compiler_params=_params("parallel"),
        name="dft_stage_c",
    )(f3, b2, u2, g2, bias_t)


def _dft_constants(seq):
    n = 2 * seq
    n2 = int(round(math.sqrt(seq)))
    n1 = n // n2
    assert n1 * n2 == n and n1 == 2 * n2
    kreal = n1 // 2 + 1
    kh = ((kreal + 7) // 8) * 8
    k1 = jnp.arange(kh, dtype=jnp.int32)
    live = (k1 < kreal)
    nn1 = jnp.arange(n1 // 2, dtype=jnp.int32)
    ph = ((k1[:, None] * nn1[None, :]) % n1).astype(F32) * (2.0 * math.pi / n1)
    lv = live[:, None].astype(F32)
    f1 = jnp.concatenate([jnp.cos(ph) * lv, -jnp.sin(ph) * lv], axis=0).astype(BF16)
    kk2 = jnp.arange(n2, dtype=jnp.int32)
    nn2 = jnp.arange(n2, dtype=jnp.int32)
    num = (k1[:, None, None] * nn2[None, None, :] + n1 * kk2[None, :, None] * nn2[None, None, :]) % n
    ang = num.astype(F32) * (2.0 * math.pi / n)
    mr, mi = jnp.cos(ang), -jnp.sin(ang)
    wf = jnp.concatenate([jnp.concatenate([mr, -mi], axis=2),
                          jnp.concatenate([mi, mr], axis=2)], axis=1).astype(BF16)
    gr, gi = jnp.swapaxes(mr, 1, 2), -jnp.swapaxes(mi, 1, 2)
    wi = jnp.concatenate([jnp.concatenate([gr, -gi], axis=2),
                          jnp.concatenate([gi, gr], axis=2)], axis=1).astype(BF16)
    ck = jnp.where((k1 == 0) | (k1 == n1 // 2), 1.0, 2.0) * live.astype(F32) / n
    ph3 = ((nn1[:, None] * k1[None, :]) % n1).astype(F32) * (2.0 * math.pi / n1)
    f3 = jnp.concatenate([jnp.cos(ph3) * ck[None, :], -jnp.sin(ph3) * ck[None, :]], axis=1).astype(BF16)
    return n1, n2, kh, f1, wf, wi, f3


def _hyena_filter_inputs(seq):
    t = jnp.linspace(0.0, 1.0, seq, dtype=F32)[:, None]
    w = 2.0 * math.pi * jnp.arange(seq, dtype=F32)[:, None] / seq
    f = jnp.linspace(1e-4, HY_BANDS - 1, HY_BANDS, dtype=F32)[None, :]
    z = jnp.concatenate([t, jnp.cos(f * w), -jnp.sin(f * w)], axis=-1)
    return jnp.pad(z, ((0, 0), (0, HY_FILTER_FF - z.shape[1])))


def _hyena_mixer(x, g, sh, sc, gate, w_in, b_in, short_w, short_b, fw1, fb1, fw2, fb2, fw3, fb3,
                 sin_freq, fw4, filter_bias, w_out, b_out):
    seq, d = x.shape
    n1, n2, kh, f1, wf, wi, f3 = _dft_constants(seq)
    kb = 8
    v, g1, g2 = _hy_in(x, g, sh, sc, w_in.astype(BF16), b_in.reshape(1, -1), short_w, short_b.reshape(1, -1),
                       tm=256)

    z = _hyena_filter_inputs(seq)
    fw1p = jnp.pad(fw1, ((0, HY_FILTER_FF - fw1.shape[0]), (0, 0)))
    deltas = jnp.linspace(math.log(HY_DECAY_TARGET) / HY_SLOW_DECAY,
                          math.log(HY_DECAY_TARGET) / HY_FAST_DECAY, d, dtype=F32)[None, :]
    hf, nrm = _hy_filter(z, fw1p, fb1.reshape(1, -1), fw2, fb2.reshape(1, -1), fw3, fb3.reshape(1, -1),
                         sin_freq.reshape(1, -1), fw4.astype(BF16), deltas, tl=512)
    nrm = nrm.reshape(2, 2, d)
    inv_norm = (1.0 / (nrm[:, 0] + nrm[:, 1])).reshape(2, 1, d)
    half = n1 // 2
    fa = _dft_a(f1, hf.reshape(half, n2 * 4 * d), tn=4096)
    kspec = _filter_spectrum(wf, fa.reshape(2, kh, n2, 4 * d), inv_norm, kb)

    def conv(u, gt, order):
        a = _dft_a(f1, u.reshape(half, n2 * d), tn=4096)
        b = _dft_mid(wf, wi, a.reshape(2, kh, n2, d), kspec, order, kb)
        tn = 2 * d
        bias_t = jnp.tile(filter_bias[order].reshape(1, d), (1, tn // d))
        zz = _dft_c(f3, b.reshape(2 * kh, n2 * d), u.reshape(half, n2 * d), gt.reshape(half, n2 * d), bias_t, tn)
        return zz.reshape(seq, d)

    z1 = conv(v, g1, 0)
    z2 = conv(z1, g2, 1)
    return _resid_proj1(z2, w_out.astype(BF16), b_out.reshape(1, -1), x, gate, tm=512)


def _attention_mixer(x, ctx, g, sh, sc, sh_c, sc_c, gate, w_in, q_norm_g, kv_norm_g, w_uq, w_ukv, sink, w_out):
    s, d = x.shape
    n_ctx = ctx.shape[0]
    w_aug, wq, wkv = _attn_weights(w_in, w_uq, w_ukv)
    cos_m, sin_m = _rope_tables(s, n_ctx, MLA_ROPE)
    tot = s + n_ctx
    cm = jnp.concatenate([jnp.ones((tot, MLA_NOPE), F32), cos_m, jnp.zeros((tot, LANES - 96), F32)], axis=1)
    sm = jnp.concatenate([jnp.zeros((tot, MLA_NOPE), F32), sin_m, jnp.zeros((tot, LANES - 96), F32)], axis=1)
    cos_g, sin_g = _rope_tables(s, n_ctx, GQA_HEAD_DIM)
    cg = jnp.concatenate([cos_g, cos_g], axis=1)
    sg = jnp.concatenate([sin_g, sin_g], axis=1)
    sh2 = jnp.stack([sh_c, sh], axis=0)
    sc2 = jnp.stack([sc_c, sc], axis=0)
    mq, mk, mv, gq, gk, gv = _attn_prep(x, ctx, g, sh2, sc2, w_aug, q_norm_g.reshape(1, -1),
                                        kv_norm_g.reshape(1, -1), wq, wkv, cm, sm, cg, sg)
    mla_out = _mla_attention(mq, mk, mv, tq=512, tk=1280 if tot % 1280 == 0 else n_ctx)
    gqa_out = _gqa_attention(sink, gq, gk, gv, n_ctx)
    wo = w_out.astype(BF16)
    nm = MLA_HEADS * MLA_V
    return _resid_proj2(mla_out, gqa_out, wo[:nm], wo[nm:], x, gate, tm=512)


def kernel(x, c, ctx, c_ctx, mod_w, mod_b, norm1_g, norm2_g, attn_w_in, mla_q_norm_g, mla_kv_norm_g,
           mla_w_uq, mla_w_ukv, gqa_sink, attn_w_out, hy_w_in, hy_b_in, hy_short_w, hy_short_b,
           hy_ffn_w1, hy_ffn_b1, hy_ffn_w2, hy_ffn_b2, hy_ffn_w3, hy_ffn_b3, hy_sin_freq, hy_ffn_w4,
           hy_filter_bias, hy_w_out, hy_b_out, moe_router_w, moe_router_b, moe_w1, moe_b1, moe_w2,
           moe_b2, final_norm_g):
    batch, seq, d = x.shape
    assert batch == 1
    depth = mod_w.shape[0]
    xs = x.reshape(seq, d)
    cs = ctx.reshape(-1, d)
    mod = _modulation(c, c_ctx, mod_w, mod_b)
    fg = final_norm_g.reshape(1, d)
    for i in range(depth):
        m = mod[i, 0].reshape(6, 1, d)
        mc = mod[i, 1].reshape(6, 1, d)
        g1 = norm1_g[i].reshape(1, d)
        j = i // 2
        if i % 2 == 0:
            xs = _attention_mixer(xs, cs, g1, m[0], m[1], mc[0], mc[1], m[2], attn_w_in[j], mla_q_norm_g[j],
                                  mla_kv_norm_g[j], mla_w_uq[j], mla_w_ukv[j], gqa_sink[j], attn_w_out[j])
        else:
            xs = _hyena_mixer(xs, g1, m[0], m[1], m[2], hy_w_in[j], hy_b_in[j], hy_short_w[j], hy_short_b[j],
                              hy_ffn_w1[j], hy_ffn_b1[j], hy_ffn_w2[j], hy_ffn_b2[j], hy_ffn_w3[j], hy_ffn_b3[j],
                              hy_sin_freq[j], hy_ffn_w4[j], hy_filter_bias[j], hy_w_out[j], hy_b_out[j])
        xs = _moe_layer(xs, norm2_g[i].reshape(1, d), m[3], m[4], m[5], moe_router_w[i], moe_router_b[i],
                        moe_w1[i], moe_b1[i], moe_w2[i], moe_b2[i], fg, final_norm=(i == depth - 1))
    return xs.reshape(batch, seq, d)
```

```python
import functools
import math

import jax
import jax.numpy as jnp
from jax import lax
from jax.experimental import pallas as pl
from jax.experimental.pallas import tpu as pltpu

F32 = jnp.float32
BF16 = jnp.bfloat16

RMS_EPS = 1e-6
ROPE_BASE = 10000.0
GRID_W = 64
NEG_INF = -1e30

MLA_HEADS = 8
MLA_Q_RANK = 256
MLA_KV_RANK = 128
MLA_NOPE = 64
MLA_ROPE = 32
MLA_V = 64
MLA_SCALE = 1.0 / math.sqrt(MLA_NOPE + MLA_ROPE)
LOG2_E = math.log2(math.e)

GQA_Q_HEADS = 8
GQA_KV_HEADS = 2
GQA_HEAD_DIM = 64
GQA_SCALE = 1.0 / math.sqrt(GQA_HEAD_DIM)
WIN_BLOCK = 128

OFF_CKV = 0
OFF_KROPE = OFF_CKV + MLA_KV_RANK
OFF_GK = OFF_KROPE + MLA_ROPE
OFF_GV = OFF_GK + GQA_KV_HEADS * GQA_HEAD_DIM
KV_COLS = OFF_GV + GQA_KV_HEADS * GQA_HEAD_DIM
OFF_CQ = KV_COLS
OFF_GQ = OFF_CQ + MLA_Q_RANK
PROJ_COLS = OFF_GQ + GQA_Q_HEADS * GQA_HEAD_DIM

HY_BANDS = 16
HY_FILTER_FF = 64
HY_DECAY_TARGET = 1e-2
HY_FAST_DECAY = 0.3
HY_SLOW_DECAY = 1.5

N_EXPERTS = 32
TOP_K = 4
SWIGLU_LIMIT = 7.0
SWIGLU_ALPHA = 1.702

LANES = 128
VMEM_LIMIT_BYTES = 56 * 1024 * 1024

C_CKV = 0
C_CQ = C_CKV + 128
C_GQ = C_CQ + 256
C_GQR = C_GQ + 512
C_GK = C_GQR + 512
C_GKR = C_GK + 512
C_GV = C_GKR + 512
C_KR = C_GV + 256
C_KRR = C_KR + 128
AUG_COLS = C_KRR + 128


def _params(*sem):
    return pltpu.CompilerParams(dimension_semantics=sem, vmem_limit_bytes=VMEM_LIMIT_BYTES)


def _const_spec(shape):
    n = len(shape)
    return pl.BlockSpec(shape, lambda *_: (0,) * n)


def _norm_mod(x, g, sh, sc):
    ms = jnp.mean(x * x, axis=-1, keepdims=True)
    return (x * lax.rsqrt(ms + RMS_EPS) * g) * (1.0 + sc) + sh


def _rms(x, g):
    ms = jnp.mean(x * x, axis=-1, keepdims=True)
    return x * lax.rsqrt(ms + RMS_EPS) * g


def _mod_kernel(cc_ref, w_ref, b_ref, o_ref):
    cc = cc_ref[...]
    s = cc / (1.0 + jnp.exp(-cc))
    w = w_ref[...]
    b = b_ref[...]
    o_ref[0:1, :] = jnp.sum(w * s[:, 0:1], axis=0, keepdims=True) + b
    o_ref[1:2, :] = jnp.sum(w * s[:, 1:2], axis=0, keepdims=True) + b


def _modulation(c, c_ctx, mod_w, mod_b):
    depth, d, n = mod_w.shape
    tn = 1536
    cc = jnp.stack([c.reshape(d), c_ctx.reshape(d)], axis=1)
    return pl.pallas_call(
        _mod_kernel,
        grid=(depth, n // tn),
        in_specs=[
            _const_spec((d, 2)),
            pl.BlockSpec((None, d, tn), lambda l, j: (l, 0, j)),
            pl.BlockSpec((None, 1, tn), lambda l, j: (l, 0, j)),
        ],
        out_specs=pl.BlockSpec((None, 2, tn), lambda l, j: (l, 0, j)),
        out_shape=jax.ShapeDtypeStruct((depth, 2, n), F32),
        compiler_params=_params("parallel", "parallel"),
        name="modulation",
    )(cc, mod_w, mod_b.reshape(depth, 1, n))


def _rot_cols_1d(w):
    half = w.shape[-1] // 2
    return jnp.concatenate([-w[..., half:], w[..., :half]], axis=-1)


def _rot_cols_2d(w):
    half = w.shape[-1] // 2
    return jnp.concatenate([_rot_cols_1d(w[..., :half]), _rot_cols_1d(w[..., half:])], axis=-1)


def _rope_tables(n_lat, n_ctx, dims):
    t = jnp.arange(n_lat, dtype=jnp.int32)
    row = (t // GRID_W).astype(F32)
    col = (t % GRID_W).astype(F32)
    q = dims // 4
    inv = ROPE_BASE ** (-jnp.arange(q, dtype=F32) / q)
    ar = row[:, None] * inv[None, :]
    ac = col[:, None] * inv[None, :]
    cos = jnp.concatenate([jnp.cos(ar), jnp.cos(ar), jnp.cos(ac), jnp.cos(ac)], axis=-1)
    sin = jnp.concatenate([jnp.sin(ar), jnp.sin(ar), jnp.sin(ac), jnp.sin(ac)], axis=-1)
    cos = jnp.concatenate([jnp.ones((n_ctx, dims), F32), cos], axis=0)
    sin = jnp.concatenate([jnp.zeros((n_ctx, dims), F32), sin], axis=0)
    return cos, sin


def _attn_weights(w_in, w_uq, w_ukv):
    d = w_in.shape[0]
    z64 = jnp.zeros((d, 64), F32)
    z32 = jnp.zeros((d, 32), F32)
    ckv = w_in[:, OFF_CKV:OFF_KROPE]
    krope = w_in[:, OFF_KROPE:OFF_GK]
    gk = w_in[:, OFF_GK:OFF_GV]
    gv = w_in[:, OFF_GV:KV_COLS]
    cq = w_in[:, OFF_CQ:OFF_GQ]
    gq = w_in[:, OFF_GQ:PROJ_COLS]
    gq_rot = _rot_cols_2d(gq.reshape(d, GQA_Q_HEADS, GQA_HEAD_DIM)).reshape(d, -1)
    k0, k1 = gk[:, :64], gk[:, 64:]
    k0r, k1r = _rot_cols_2d(k0), _rot_cols_2d(k1)
    gk4 = jnp.concatenate([k0, z64, z64, k0, k1, z64, z64, k1], axis=1)
    gk4r = jnp.concatenate([k0r, z64, z64, k0r, k1r, z64, z64, k1r], axis=1)
    v0, v1 = gv[:, :64], gv[:, 64:]
    gv2 = jnp.concatenate([v0, v0, v1, v1], axis=1)
    kr = jnp.concatenate([z64, krope, z32], axis=1)
    krr = jnp.concatenate([z64, _rot_cols_2d(krope), z32], axis=1)
    w_aug = jnp.concatenate([ckv, cq, gq, gq_rot, gk4, gk4r, gv2, kr, krr], axis=1).astype(BF16)

    r = w_uq.shape[0]
    q3 = w_uq.reshape(r, MLA_HEADS, MLA_NOPE + MLA_ROPE)
    q_main = jnp.pad(q3, ((0, 0), (0, 0), (0, LANES - MLA_NOPE - MLA_ROPE)))
    q_rot = jnp.concatenate([jnp.zeros((r, MLA_HEADS, MLA_NOPE), F32), _rot_cols_2d(q3[..., MLA_NOPE:]),
                             jnp.zeros((r, MLA_HEADS, LANES - MLA_NOPE - MLA_ROPE), F32)], axis=-1)
    wq = jnp.concatenate([q_main.reshape(r, -1), q_rot.reshape(r, -1)], axis=1).astype(BF16)

    rk = w_ukv.shape[0]
    kv3 = w_ukv.reshape(rk, MLA_HEADS, MLA_NOPE + MLA_V)
    k_part = jnp.pad(kv3[..., :MLA_NOPE], ((0, 0), (0, 0), (0, LANES - MLA_NOPE))).reshape(rk, -1)
    v_part = kv3[..., MLA_NOPE:].reshape(rk, -1)
    wkv = jnp.concatenate([k_part, v_part], axis=1).astype(BF16)
    return w_aug, wq, wkv


def _attn_prep_kernel(x_ref, ctx_ref, g_ref, sh_ref, sc_ref, w_ref, gq_ref, gkv_ref, wq_ref, wkv_ref,
                      cm_ref, sm_ref, cg_ref, sg_ref,
                      mq_ref, mk_ref, mv_ref, gqo_ref, gko_ref, gvo_ref):
    i = pl.program_id(0)
    xin = jnp.where(i == 0, ctx_ref[...], x_ref[...])
    h = _norm_mod(xin, g_ref[...], sh_ref[...], sc_ref[...])
    p = jnp.dot(h.astype(BF16), w_ref[...], preferred_element_type=F32)

    ckv = _rms(p[:, C_CKV:C_CKV + 128], gkv_ref[...])
    kv = jnp.dot(ckv.astype(BF16), wkv_ref[...], preferred_element_type=F32)
    cq = _rms(p[:, C_CQ:C_CQ + 256], gq_ref[...])
    qa = jnp.dot(cq.astype(BF16), wq_ref[...], preferred_element_type=F32)

    cm, sm = cm_ref[...], sm_ref[...]
    kr = p[:, C_KR:C_KR + 128] * cm + p[:, C_KRR:C_KRR + 128] * sm
    for hd in range(MLA_HEADS):
        lo = hd * LANES
        qh = qa[:, lo:lo + LANES] * cm + qa[:, 1024 + lo:1024 + lo + LANES] * sm
        mq_ref[hd] = (qh * (MLA_SCALE * LOG2_E)).astype(BF16)
        mk_ref[hd] = (kv[:, lo:lo + LANES] + kr).astype(BF16)
    mv_ref[...] = kv[:, 1024:1536].astype(BF16)

    cg, sg = cg_ref[...], sg_ref[...]
    for gI in range(4):
        lo = gI * LANES
        gq = p[:, C_GQ + lo:C_GQ + lo + LANES] * cg + p[:, C_GQR + lo:C_GQR + lo + LANES] * sg
        gqo_ref[:, lo:lo + LANES] = (gq * GQA_SCALE).astype(BF16)
        gk = p[:, C_GK + lo:C_GK + lo + LANES] * cg + p[:, C_GKR + lo:C_GKR + lo + LANES] * sg
        gko_ref[:, lo:lo + LANES] = gk.astype(BF16)
    gvo_ref[...] = p[:, C_GV:C_GV + 256].astype(BF16)


def _attn_prep(x, ctx, g, sh2, sc2, w_aug, gq, gkv, wq, wkv, cm, sm, cg, sg):
    s, d = x.shape
    c = ctx.shape[0]
    tm = c
    nt = s // tm + 1
    tot = s + c
    lat = lambda i: (jnp.maximum(i - 1, 0), 0)
    row = lambda i: (i, 0)
    return pl.pallas_call(
        _attn_prep_kernel,
        grid=(nt,),
        in_specs=[
            pl.BlockSpec((tm, d), lat),
            _const_spec((c, d)),
            _const_spec((1, d)),
            pl.BlockSpec((None, 1, d), lambda i: (jnp.minimum(i, 1), 0, 0)),
            pl.BlockSpec((None, 1, d), lambda i: (jnp.minimum(i, 1), 0, 0)),
            _const_spec(w_aug.shape),
            _const_spec(gq.shape),
            _const_spec(gkv.shape),
            _const_spec(wq.shape),
            _const_spec(wkv.shape),
            pl.BlockSpec((tm, LANES), row),
            pl.BlockSpec((tm, LANES), row),
            pl.BlockSpec((tm, LANES), row),
            pl.BlockSpec((tm, LANES), row),
        ],
        out_specs=[
            pl.BlockSpec((MLA_HEADS, tm, LANES), lambda i: (0, jnp.maximum(i - 1, 0), 0)),
            pl.BlockSpec((MLA_HEADS, tm, LANES), lambda i: (0, i, 0)),
            pl.BlockSpec((tm, 512), row),
            pl.BlockSpec((tm, 512), lat),
            pl.BlockSpec((tm, 512), row),
            pl.BlockSpec((tm, 256), row),
        ],
        out_shape=[
            jax.ShapeDtypeStruct((MLA_HEADS, s, LANES), BF16),
            jax.ShapeDtypeStruct((MLA_HEADS, tot, LANES), BF16),
            jax.ShapeDtypeStruct((tot, 512), BF16),
            jax.ShapeDtypeStruct((s, 512), BF16),
            jax.ShapeDtypeStruct((tot, 512), BF16),
            jax.ShapeDtypeStruct((tot, 256), BF16),
        ],
        compiler_params=_params("arbitrary"),
        name="attn_prep",
    )(x, ctx, g, sh2, sc2, w_aug, gq, gkv, wq, wkv, cm, sm, cg, sg)


def _mla_kernel(q_ref, k_ref, v_ref, o_ref, m_sc, acc_sc):
    j = pl.program_id(2)

    @pl.when(j == 0)
    def _():
        m_sc[...] = jnp.full(m_sc.shape, -jnp.inf, F32)
        acc_sc[...] = jnp.zeros(acc_sc.shape, F32)

    v = v_ref[...]
    v1 = jnp.concatenate([v, jnp.ones(v.shape, BF16)], axis=1)
    for hh in range(2):
        s = lax.dot_general(q_ref[hh], k_ref[hh], (((1,), (1,)), ((), ())),
                            preferred_element_type=F32)
        m_prev = m_sc[hh]
        m_new = jnp.maximum(m_prev, jnp.max(s, axis=-1, keepdims=True))
        alpha = jnp.exp2(m_prev - m_new)
        p = jnp.exp2(s - m_new).astype(BF16)
        acc_sc[hh] = alpha * acc_sc[hh] + jnp.dot(p, v1, preferred_element_type=F32)
        m_sc[hh] = m_new

    @pl.when(j == pl.num_programs(2) - 1)
    def _():
        lane = lax.broadcasted_iota(jnp.int32, (acc_sc.shape[1], LANES), 1)
        o0 = acc_sc[0, :, :LANES] / acc_sc[0, :, LANES:]
        o1 = acc_sc[1, :, :LANES] / acc_sc[1, :, LANES:]
        o_ref[...] = jnp.where(lane < MLA_V, o0, o1).astype(o_ref.dtype)


def _mla_attention(mq, mk, mv, tq, tk):
    _, s, _ = mq.shape
    tot = mk.shape[1]
    return pl.pallas_call(
        _mla_kernel,
        grid=(MLA_HEADS // 2, s // tq, tot // tk),
        in_specs=[
            pl.BlockSpec((2, tq, LANES), lambda h, i, j: (h, i, 0)),
            pl.BlockSpec((2, tk, LANES), lambda h, i, j: (h, j, 0)),
            pl.BlockSpec((tk, LANES), lambda h, i, j: (j, h)),
        ],
        out_specs=pl.BlockSpec((tq, LANES), lambda h, i, j: (i, h)),
        out_shape=jax.ShapeDtypeStruct((s, MLA_HEADS * MLA_V), BF16),
        scratch_shapes=[
            pltpu.VMEM((2, tq, 1), F32),
            pltpu.VMEM((2, tq, 2 * LANES), F32),
        ],
        compiler_params=_params("parallel", "parallel", "arbitrary"),
        name="mla_flash",
    )(mq, mk, mv)


def _gqa_kernel(sink_ref, q_ref, kp_ref, ko_ref, kn_ref, kc_ref, vp_ref, vo_ref, vn_ref, vc_ref, o_ref):
    n = pl.program_id(0)
    nb = pl.num_programs(0)
    wb = WIN_BLOCK
    r = lax.broadcasted_iota(jnp.int32, (wb, wb), 0)
    c = lax.broadcasted_iota(jnp.int32, (wb, wb), 1)
    ok_prev = c >= r + jnp.where(n >= 1, 0, wb)
    ok_next = c + jnp.where(n <= nb - 2, 0, wb) <= r
    lane = lax.broadcasted_iota(jnp.int32, (wb, LANES), 1)
    dn = (((1,), (1,)), ((), ()))
    for pr in range(GQA_Q_HEADS // 2):
        kvh = pr // 2
        q = q_ref[:, pr * LANES:(pr + 1) * LANES]
        vlo = kvh * LANES
        outs = []
        for var in range(2):
            hd = 2 * pr + var
            klo = (2 * kvh + var) * LANES
            s_p = lax.dot_general(q, kp_ref[:, klo:klo + LANES], dn, preferred_element_type=F32)
            s_o = lax.dot_general(q, ko_ref[:, klo:klo + LANES], dn, preferred_element_type=F32)
            s_n = lax.dot_general(q, kn_ref[:, klo:klo + LANES], dn, preferred_element_type=F32)
            s_c = lax.dot_general(q, kc_ref[:, klo:klo + LANES], dn, preferred_element_type=F32)
            s_p = jnp.where(ok_prev, s_p, NEG_INF)
            s_n = jnp.where(ok_next, s_n, NEG_INF)
            sink = sink_ref[hd]
            m = jnp.maximum(jnp.maximum(jnp.max(s_p, axis=-1, keepdims=True),
                                        jnp.max(s_o, axis=-1, keepdims=True)),
                            jnp.maximum(jnp.max(s_n, axis=-1, keepdims=True),
                                        jnp.max(s_c, axis=-1, keepdims=True)))
            m = jnp.maximum(m, sink)
            e_p, e_o, e_n, e_c = jnp.exp(s_p - m), jnp.exp(s_o - m), jnp.exp(s_n - m), jnp.exp(s_c - m)
            den = (jnp.sum(e_p, axis=-1, keepdims=True) + jnp.sum(e_o, axis=-1, keepdims=True)
                   + jnp.sum(e_n, axis=-1, keepdims=True) + jnp.sum(e_c, axis=-1, keepdims=True)
                   + jnp.exp(sink - m))
            o = (jnp.dot(e_p.astype(BF16), vp_ref[:, vlo:vlo + LANES], preferred_element_type=F32)
                 + jnp.dot(e_o.astype(BF16), vo_ref[:, vlo:vlo + LANES], preferred_element_type=F32)
                 + jnp.dot(e_n.astype(BF16), vn_ref[:, vlo:vlo + LANES], preferred_element_type=F32)
                 + jnp.dot(e_c.astype(BF16), vc_ref[:, vlo:vlo + LANES], preferred_element_type=F32))
            outs.append(o / den)
        o_ref[:, pr * LANES:(pr + 1) * LANES] = jnp.where(lane < GQA_HEAD_DIM, outs[0], outs[1]).astype(o_ref.dtype)


def _gqa_attention(sink, gq, gk, gv, n_ctx):
    s = gq.shape[0]
    wb = WIN_BLOCK
    nb = s // wb
    cb = n_ctx // wb
    prev = lambda n: (n + cb - 1, 0)
    own = lambda n: (n + cb, 0)
    nxt = lambda n: (jnp.minimum(n + cb + 1, nb + cb - 1), 0)
    return pl.pallas_call(
        _gqa_kernel,
        grid=(nb,),
        in_specs=[
            pl.BlockSpec(memory_space=pltpu.SMEM),
            pl.BlockSpec((wb, 512), lambda n: (n, 0)),
            pl.BlockSpec((wb, 512), prev),
            pl.BlockSpec((wb, 512), own),
            pl.BlockSpec((wb, 512), nxt),
            _const_spec((n_ctx, 512)),
            pl.BlockSpec((wb, 256), prev),
            pl.BlockSpec((wb, 256), own),
            pl.BlockSpec((wb, 256), nxt),
            _const_spec((n_ctx, 256)),
        ],
        out_specs=pl.BlockSpec((wb, 512), lambda n: (n, 0)),
        out_shape=jax.ShapeDtypeStruct((s, 512), BF16),
        compiler_params=_params("parallel"),
        name="gqa_window",
    )(sink, gq, gk, gk, gk, gk, gv, gv, gv, gv)


def _resid_proj2_kernel(a_ref, b_ref, wa_ref, wb_ref, x_ref, gt_ref, o_ref):
    y = (jnp.dot(a_ref[...], wa_ref[...], preferred_element_type=F32)
         + jnp.dot(b_ref[...], wb_ref[...], preferred_element_type=F32))
    o_ref[...] = x_ref[...] + gt_ref[...] * y


def _resid_proj2(a, b, wa, wb, x, gate, tm):
    s, d = x.shape
    row = lambda i: (i, 0)
    return pl.pallas_call(
        _resid_proj2_kernel,
        grid=(s // tm,),
        in_specs=[pl.BlockSpec((tm, a.shape[1]), row), pl.BlockSpec((tm, b.shape[1]), row),
                  _const_spec(wa.shape), _const_spec(wb.shape),
                  pl.BlockSpec((tm, d), row), _const_spec((1, d))],
        out_specs=pl.BlockSpec((tm, d), row),
        out_shape=jax.ShapeDtypeStruct((s, d), F32),
        compiler_params=_params("parallel"),
        name="attn_out_proj",
    )(a, b, wa, wb, x, gate)


def _resid_proj1_kernel(a_ref, w_ref, bias_ref, x_ref, gt_ref, o_ref):
    y = jnp.dot(a_ref[...], w_ref[...], preferred_element_type=F32) + bias_ref[...]
    o_ref[...] = x_ref[...] + gt_ref[...] * y


def _resid_proj1(a, w, bias, x, gate, tm):
    s, d = x.shape
    row = lambda i: (i, 0)
    return pl.pallas_call(
        _resid_proj1_kernel,
        grid=(s // tm,),
        in_specs=[pl.BlockSpec((tm, a.shape[1]), row), _const_spec(w.shape), _const_spec((1, d)),
                  pl.BlockSpec((tm, d), row), _const_spec((1, d))],
        out_specs=pl.BlockSpec((tm, d), row),
        out_shape=jax.ShapeDtypeStruct((s, d), F32),
        compiler_params=_params("parallel"),
        name="hyena_out_proj",
    )(a, w, bias, x, gate)


META_IDX, META_W, META_RANK = 0, 4, 8


def _router_kernel(x_ref, g_ref, sh_ref, sc_ref, rw_ref, rb_ref, t_ref, meta_ref, cnt_ref, carry_sc):
    i = pl.program_id(0)

    @pl.when(i == 0)
    def _():
        carry_sc[...] = jnp.zeros(carry_sc.shape, F32)

    t = _norm_mod(x_ref[...], g_ref[...], sh_ref[...], sc_ref[...])
    t_ref[...] = t
    logits = jnp.dot(t, rw_ref[...], preferred_element_type=F32,
                     precision=lax.Precision.HIGHEST) + rb_ref[...]
    tm = logits.shape[0]
    lane = lax.broadcasted_iota(jnp.int32, logits.shape, 1)
    work = logits
    vals, hots, idxs = [], [], []
    for _ in range(TOP_K):
        m = jnp.max(work, axis=-1, keepdims=True)
        idx = jnp.min(jnp.where(work == m, lane, N_EXPERTS), axis=-1, keepdims=True)
        hot = lane == idx
        vals.append(m)
        hots.append(hot)
        idxs.append(idx)
        work = jnp.where(hot, -jnp.inf, work)
    es = [jnp.exp(v - vals[0]) for v in vals]
    den = es[0] + es[1] + es[2] + es[3]
    hot_all = jnp.zeros(logits.shape, F32)
    for hot in hots:
        hot_all = hot_all + jnp.where(hot, 1.0, 0.0)
    r = lax.broadcasted_iota(jnp.int32, (tm, tm), 0)
    c = lax.broadcasted_iota(jnp.int32, (tm, tm), 1)
    tri = jnp.where(r > c, 1.0, 0.0).astype(BF16)
    rank = jnp.dot(tri, hot_all.astype(BF16), preferred_element_type=F32) + carry_sc[...]
    carry_sc[...] += jnp.sum(hot_all, axis=0, keepdims=True)
    cnt_ref[...] = carry_sc[...]
    lane_m = lax.broadcasted_iota(jnp.int32, (tm, LANES), 1)
    meta = jnp.zeros((tm, LANES), F32)
    for k in range(TOP_K):
        rk = jnp.sum(jnp.where(hots[k], rank, 0.0), axis=-1, keepdims=True)
        meta = jnp.where(lane_m == META_IDX + k, idxs[k].astype(F32), meta)
        meta = jnp.where(lane_m == META_W + k, es[k] / den, meta)
        meta = jnp.where(lane_m == META_RANK + k, rk, meta)
    meta_ref[...] = meta


def _router(x, g, sh, sc, rw, rb, tm):
    s, d = x.shape
    e = rw.shape[1]
    row = lambda i: (i, 0)
    return pl.pallas_call(
        _router_kernel,
        grid=(s // tm,),
        in_specs=[pl.BlockSpec((tm, d), row), _const_spec((1, d)), _const_spec((1, d)), _const_spec((1, d)),
                  _const_spec((d, e)), _const_spec((1, e))],
        out_specs=[pl.BlockSpec((tm, d), row), pl.BlockSpec((tm, LANES), row), _const_spec((1, e))],
        out_shape=[jax.ShapeDtypeStruct((s, d), F32), jax.ShapeDtypeStruct((s, LANES), F32),
                   jax.ShapeDtypeStruct((1, e), F32)],
        scratch_shapes=[pltpu.VMEM((1, e), F32)],
        compiler_params=_params("arbitrary"),
        name="moe_router",
    )(x, g, sh, sc, rw, rb)


def _rows_copy(src, dst, sem, n):
    return pltpu.make_async_copy(src.at[pl.ds(0, n)], dst.at[pl.ds(0, n)], sem)


def _dispatch_kernel(pos_ref, t_ref, xs_init_ref, xs_ref, sem, *, tm):
    del xs_init_ref
    base = pl.program_id(0) * tm

    def body(r, carry):
        for k in range(TOP_K):
            p = pos_ref[(base + r) * TOP_K + k]
            pltpu.make_async_copy(t_ref.at[pl.ds(r, 1)], xs_ref.at[pl.ds(p, 1)], sem).start(priority=k % 2)
        return carry

    lax.fori_loop(0, tm, body, 0, unroll=8)
    for k in range(TOP_K):
        _rows_copy(t_ref, xs_ref, sem, tm).wait()


def _dispatch(pos, t, rows, tm):
    s, d = t.shape
    grid_spec = pltpu.PrefetchScalarGridSpec(
        num_scalar_prefetch=1,
        grid=(s // tm,),
        in_specs=[pl.BlockSpec((tm, d), lambda i, pos: (i, 0)),
                  pl.BlockSpec(memory_space=pl.ANY)],
        out_specs=pl.BlockSpec(memory_space=pl.ANY),
        scratch_shapes=[pltpu.SemaphoreType.DMA(())],
    )
    return pl.pallas_call(
        functools.partial(_dispatch_kernel, tm=tm),
        grid_spec=grid_spec,
        out_shape=jax.ShapeDtypeStruct((rows, d), F32),
        input_output_aliases={2: 0},
        compiler_params=_params("arbitrary"),
        name="moe_dispatch",
    )(pos, t, jnp.zeros((rows, d), F32))


def _moe_expert_kernel(te_ref, nu_ref, xs_ref, w1_ref, b1_ref, w2_ref, b2_ref, y_ref, w1b, w2b, *, ff):
    i = pl.program_id(0)
    e = te_ref[i]
    e_prev = te_ref[jnp.maximum(i - 1, 0)]

    @pl.when((i == 0) | (e != e_prev))
    def _():
        w1b[...] = w1_ref[...].astype(BF16)
        w2b[...] = w2_ref[...].astype(BF16)

    @pl.when(i < nu_ref[0])
    def _():
        u = jnp.dot(xs_ref[...].astype(BF16), w1b[...], preferred_element_type=F32) + b1_ref[...]
        glu = jnp.minimum(u[:, :ff], SWIGLU_LIMIT)
        lin = jnp.clip(u[:, ff:], -SWIGLU_LIMIT, SWIGLU_LIMIT)
        act = glu / (1.0 + jnp.exp(-SWIGLU_ALPHA * glu)) * (lin + 1.0)
        y_ref[...] = jnp.dot(act.astype(BF16), w2b[...], preferred_element_type=F32) + b2_ref[...]

    @pl.when(i >= nu_ref[0])
    def _():
        y_ref[...] = jnp.zeros(y_ref.shape, F32)


def _moe_experts(tile_expert, n_used, xs, w1, b1, w2, b2, tmm, layer):
    rows, d = xs.shape
    _, ne, _, ff2 = w1.shape
    ff = ff2 // 2
    wsel = lambda i, te, nu: (layer, te[i], 0, 0)
    grid_spec = pltpu.PrefetchScalarGridSpec(
        num_scalar_prefetch=2,
        grid=(rows // tmm,),
        in_specs=[pl.BlockSpec((tmm, d), lambda i, te, nu: (jnp.minimum(i, nu[0] - 1), 0)),
                  pl.BlockSpec((None, None, d, ff2), wsel),
                  pl.BlockSpec((None, None, 1, ff2), wsel),
                  pl.BlockSpec((None, None, ff, d), wsel),
                  pl.BlockSpec((None, None, 1, d), wsel)],
        out_specs=pl.BlockSpec((tmm, d), lambda i, te, nu: (i, 0)),
        scratch_shapes=[pltpu.VMEM((d, ff2), BF16), pltpu.VMEM((ff, d), BF16)],
    )
    return pl.pallas_call(
        functools.partial(_moe_expert_kernel, ff=ff),
        grid_spec=grid_spec,
        out_shape=jax.ShapeDtypeStruct((rows, d), F32),
        compiler_params=_params("arbitrary"),
        name="moe_experts",
    )(tile_expert, n_used, xs, w1, b1.reshape(-1, ne, 1, ff2), w2, b2.reshape(-1, ne, 1, d))


def _combine_kernel(pos_ref, y_ref, meta_ref, x_ref, gt_ref, fg_ref, o_ref, buf, sem, *, tc, final_norm):
    base = pl.program_id(0) * tc

    def body(r, carry):
        for k in range(TOP_K):
            p = pos_ref[(base + r) * TOP_K + k]
            pltpu.make_async_copy(y_ref.at[pl.ds(p, 1)], buf.at[k, pl.ds(r, 1)], sem).start(priority=k % 2)
        return carry

    lax.fori_loop(0, tc, body, 0, unroll=8)
    for k in range(TOP_K):
        _rows_copy(y_ref, buf.at[k], sem, tc).wait()
    meta = meta_ref[...]
    acc = buf[0] * meta[:, META_W:META_W + 1]
    for k in range(1, TOP_K):
        acc = acc + buf[k] * meta[:, META_W + k:META_W + k + 1]
    xo = x_ref[...] + gt_ref[...] * acc
    if final_norm:
        xo = _rms(xo, fg_ref[...])
    o_ref[...] = xo


def _combine(pos, y, meta, x, gate, final_g, tc, final_norm):
    s, d = x.shape
    row = lambda i, pos: (i, 0)
    grid_spec = pltpu.PrefetchScalarGridSpec(
        num_scalar_prefetch=1,
        grid=(s // tc,),
        in_specs=[pl.BlockSpec(memory_space=pl.ANY),
                  pl.BlockSpec((tc, LANES), row),
                  pl.BlockSpec((tc, d), row),
                  pl.BlockSpec((1, d), lambda i, pos: (0, 0)),
                  pl.BlockSpec((1, d), lambda i, pos: (0, 0))],
        out_specs=pl.BlockSpec((tc, d), row),
        scratch_shapes=[pltpu.VMEM((TOP_K, tc, d), F32), pltpu.SemaphoreType.DMA(())],
    )
    return pl.pallas_call(
        functools.partial(_combine_kernel, tc=tc, final_norm=final_norm),
        grid_spec=grid_spec,
        out_shape=jax.ShapeDtypeStruct((s, d), F32),
        compiler_params=_params("arbitrary"),
        name="moe_combine",
    )(pos, y, meta, x, gate, final_g)


MOE_ROW_TILE = 256


def _moe_layer(x, g, sh, sc, gate, rw, rb, w1, b1, w2, b2, final_g, final_norm, layer):
    s, d = x.shape
    ne = rw.shape[1]
    tmm = MOE_ROW_TILE
    t, meta, cnt = _router(x, g, sh, sc, rw, rb.reshape(1, -1), tm=512)
    idx = meta[:, META_IDX:META_IDX + TOP_K].astype(jnp.int32)
    rank = meta[:, META_RANK:META_RANK + TOP_K].astype(jnp.int32)
    counts = cnt[0].astype(jnp.int32)
    padded = ((counts + tmm - 1) // tmm) * tmm
    ends = jnp.cumsum(padded)
    pos = (jnp.take(ends - padded, idx) + rank).reshape(-1)
    rows = TOP_K * s + ne * tmm
    nt = rows // tmm
    tile_ends = ends // tmm
    n_used = tile_ends[-1]
    tiles = jnp.arange(nt, dtype=jnp.int32)
    last = jnp.maximum(n_used - 1, 0)
    te = jnp.sum((jnp.minimum(tiles, last)[:, None] >= tile_ends[None, :]).astype(jnp.int32), axis=1)
    te = jnp.minimum(te, ne - 1)
    xs = _dispatch(pos, t, rows, tm=512)
    y = _moe_experts(te, n_used.reshape(1), xs, w1, b1, w2, b2, tmm, layer)
    return _combine(pos, y, meta, x, gate, final_g, tc=256, final_norm=final_norm)


def _hy_in_kernel(xp_ref, x_ref, xn_ref, g_ref, sh_ref, sc_ref, w_ref, b_ref, sw_ref, sb_ref,
                  v_ref, g1_ref, g2_ref, *, tm, seq):
    i = pl.program_id(0)
    d = x_ref.shape[1]
    xe = jnp.concatenate([xp_ref[...], x_ref[...], xn_ref[...]], axis=0)
    h = _norm_mod(xe, g_ref[...], sh_ref[...], sc_ref[...])
    p = jnp.dot(h.astype(BF16), w_ref[...], preferred_element_type=F32) + b_ref[...]
    row = i * tm - 8 + lax.broadcasted_iota(jnp.int32, (tm + 16, 1), 0)
    p = jnp.where((row >= 0) & (row < seq), p, 0.0)
    pm = pltpu.roll(p, 1, 0)[8:8 + tm]
    pc = p[8:8 + tm]
    pn = pltpu.roll(p, tm + 15, 0)[8:8 + tm]
    sw = sw_ref[...]
    out = sw[0:1] * pm + sw[1:2] * pc + sw[2:3] * pn + sb_ref[...]
    v_ref[...] = out[:, :d].astype(BF16)
    g1_ref[...] = out[:, d:2 * d].astype(BF16)
    g2_ref[...] = out[:, 2 * d:].astype(BF16)


def _hy_in(x, g, sh, sc, w, b, sw, sb, tm):
    s, d = x.shape
    n3 = w.shape[1]
    r8 = tm // 8
    nblk8 = s // 8
    row = lambda i: (i, 0)
    out = jax.ShapeDtypeStruct((s, d), BF16)
    return pl.pallas_call(
        functools.partial(_hy_in_kernel, tm=tm, seq=s),
        grid=(s // tm,),
        in_specs=[
            pl.BlockSpec((8, d), lambda i: (jnp.maximum(i * r8 - 1, 0), 0)),
            pl.BlockSpec((tm, d), row),
            pl.BlockSpec((8, d), lambda i: (jnp.minimum((i + 1) * r8, nblk8 - 1), 0)),
            _const_spec((1, d)), _const_spec((1, d)), _const_spec((1, d)),
            _const_spec((d, n3)), _const_spec((1, n3)), _const_spec((3, n3)), _const_spec((1, n3)),
        ],
        out_specs=[pl.BlockSpec((tm, d), row)] * 3,
        out_shape=[out, out, out],
        compiler_params=_params("parallel"),
        name="hyena_in_proj",
    )(x, x, x, g, sh, sc, w, b, sw, sb)


def _hy_filter_kernel(z_ref, w1_ref, b1_ref, w2_ref, b2_ref, w3_ref, b3_ref, fr_ref, w4_ref, dl_ref,
                      hf_ref, nrm_ref):
    i = pl.program_id(0)
    hp = lax.Precision.HIGHEST
    z = z_ref[...]
    fr = fr_ref[...]
    a = jnp.sin(fr * (jnp.dot(z, w1_ref[...], preferred_element_type=F32, precision=hp) + b1_ref[...]))
    a = jnp.sin(fr * (jnp.dot(a, w2_ref[...], preferred_element_type=F32, precision=hp) + b2_ref[...]))
    a = jnp.sin(fr * (jnp.dot(a, w3_ref[...], preferred_element_type=F32, precision=hp) + b3_ref[...]))
    hf = jnp.dot(a.astype(BF16), w4_ref[...], preferred_element_type=F32)
    decay = jnp.exp(-z[:, 0:1] * jnp.abs(dl_ref[...]))
    d = decay.shape[1]
    cols = []
    for q in range(4):
        cols.append(hf[:, q * d:(q + 1) * d] * decay)
    hf = jnp.concatenate(cols, axis=1)

    @pl.when(i == 0)
    def _():
        nrm_ref[...] = jnp.zeros(nrm_ref.shape, F32)

    nrm_ref[...] += jnp.sum(jnp.abs(hf), axis=0, keepdims=True)
    t0 = (i == 0) & (lax.broadcasted_iota(jnp.int32, hf.shape, 0) == 0)
    col = lax.broadcasted_iota(jnp.int32, hf.shape, 1)
    bwd = ((col >= d) & (col < 2 * d)) | (col >= 3 * d)
    hf_ref[...] = jnp.where(t0 & bwd, 0.0, hf).astype(BF16)


def _hy_filter(z, w1, b1, w2, b2, w3, b3, fr, w4, deltas, tl):
    seq = z.shape[0]
    n4 = w4.shape[1]
    ff = w2.shape[0]
    return pl.pallas_call(
        _hy_filter_kernel,
        grid=(seq // tl,),
        in_specs=[pl.BlockSpec((tl, z.shape[1]), lambda i: (i, 0)),
                  _const_spec(w1.shape), _const_spec((1, ff)), _const_spec(w2.shape), _const_spec((1, ff)),
                  _const_spec(w3.shape), _const_spec((1, ff)), _const_spec((1, ff)), _const_spec(w4.shape),
                  _const_spec(deltas.shape)],
        out_specs=[pl.BlockSpec((tl, n4), lambda i: (i, 0)), _const_spec((1, n4))],
        out_shape=[jax.ShapeDtypeStruct((seq, n4), BF16), jax.ShapeDtypeStruct((1, n4), F32)],
        compiler_params=_params("arbitrary"),
        name="hyena_filter_mlp",
    )(z, w1, b1, w2, b2, w3, b3, fr, w4, deltas)


def _dft_a_kernel(f_ref, u_ref, o_ref):
    o_ref[...] = jnp.dot(f_ref[...], u_ref[...], preferred_element_type=F32).astype(o_ref.dtype)


def _dft_a(f1, u2, tn):
    m, k = f1.shape
    cols = u2.shape[1]
    return pl.pallas_call(
        _dft_a_kernel,
        grid=(cols // tn,),
        in_specs=[_const_spec((m, k)), pl.BlockSpec((k, tn), lambda j: (0, j))],
        out_specs=pl.BlockSpec((m, tn), lambda j: (0, j)),
        out_shape=jax.ShapeDtypeStruct((m, cols), BF16),
        compiler_params=_params("parallel"),
        name="dft_stage_a",
    )(f1, u2)


def _spectrum_kernel(wf_ref, a_ref, inv_ref, k_ref, *, kb, n2):
    inv = inv_ref[...]
    d = inv.shape[1]
    for b in range(kb):
        a = a_ref[:, b].reshape(2 * n2, 2 * d)
        x = jnp.dot(wf_ref[b], a, preferred_element_type=F32)
        xr, xi = x[:n2], x[n2:]
        k_ref[0, b] = (xr[:, :d] + xr[:, d:]) * inv
        k_ref[1, b] = (xi[:, :d] - xi[:, d:]) * inv


def _filter_spectrum(wf, a4, inv_norm, kb):
    _, kh, n2, c4 = a4.shape
    d = c4 // 4
    return pl.pallas_call(
        functools.partial(_spectrum_kernel, kb=kb, n2=n2),
        grid=(2, kh // kb),
        in_specs=[pl.BlockSpec((kb, 2 * n2, 2 * n2), lambda o, j: (j, 0, 0)),
                  pl.BlockSpec((2, kb, n2, 2 * d), lambda o, j: (0, j, 0, o)),
                  pl.BlockSpec((None, 1, d), lambda o, j: (o, 0, 0))],
        out_specs=pl.BlockSpec((None, 2, kb, n2, d), lambda o, j: (o, 0, j, 0, 0)),
        out_shape=jax.ShapeDtypeStruct((2, 2, kh, n2, d), F32),
        compiler_params=_params("parallel", "parallel"),
        name="filter_spectrum",
    )(wf, a4, inv_norm)


def _dft_mid_kernel(wf_ref, wi_ref, a_ref, k_ref, o_ref, *, kb, n2):
    c = a_ref.shape[-1]
    for b in range(kb):
        a = a_ref[:, b].reshape(2 * n2, c)
        x = jnp.dot(wf_ref[b], a, preferred_element_type=F32)
        xr, xi = x[:n2], x[n2:]
        kr, ki = k_ref[0, b], k_ref[1, b]
        y = jnp.concatenate([xr * kr - xi * ki, xr * ki + xi * kr], axis=0).astype(BF16)
        bb = jnp.dot(wi_ref[b], y, preferred_element_type=F32)
        o_ref[:, b] = bb.reshape(2, n2, c).astype(o_ref.dtype)


def _dft_mid(wf, wi, a4, kspec, order, kb):
    _, kh, n2, c = a4.shape
    return pl.pallas_call(
        functools.partial(_dft_mid_kernel, kb=kb, n2=n2),
        grid=(kh // kb,),
        in_specs=[pl.BlockSpec((kb, 2 * n2, 2 * n2), lambda j: (j, 0, 0)),
                  pl.BlockSpec((kb, 2 * n2, 2 * n2), lambda j: (j, 0, 0)),
                  pl.BlockSpec((2, kb, n2, c), lambda j: (0, j, 0, 0)),
                  pl.BlockSpec((None, 2, kb, n2, c), lambda j: (order, 0, j, 0, 0))],
        out_specs=pl.BlockSpec((2, kb, n2, c), lambda j: (0, j, 0, 0)),
        out_shape=jax.ShapeDtypeStruct(a4.shape, BF16),
        compiler_params=_params("parallel"),
        name="dft_middle",
    )(wf, wi, a4, kspec)


def _dft_c_kernel(f_ref, b_ref, u_ref, g_ref, bias_ref, o_ref):
    y = jnp.dot(f_ref[...], b_ref[...], preferred_element_type=F32)
    u = u_ref[...].astype(F32)
    o_ref[...] = (g_ref[...].astype(F32) * (y + u * bias_ref[...])).astype(o_ref.dtype)


def _dft_c(f3, b2, u2, g2, bias_t, tn):
    m, k = f3.shape
    cols = b2.shape[1]
    col = lambda j: (0, j)
    return pl.pallas_call(
        _dft_c_kernel,
        grid=(cols // tn,),
        in_specs=[_const_spec((m, k)), pl.BlockSpec((k, tn), col), pl.BlockSpec((m, tn), col),
                  pl.BlockSpec((m, tn), col), _const_spec((1, tn))],
        out_specs=pl.BlockSpec((m, tn), col),
        out_shape=jax.ShapeDtypeStruct((m, cols), BF16),
        compiler_params=_params("parallel"),
        name="dft_stage_c",
    )(f3, b2, u2, g2, bias_t)


def _dft_constants(seq):
    n = 2 * seq
    n2 = int(round(math.sqrt(seq)))
    n1 = n // n2
    assert n1 * n2 == n and n1 == 2 * n2
    kreal = n1 // 2 + 1
    kh = ((kreal + 7) // 8) * 8
    k1 = jnp.arange(kh, dtype=jnp.int32)
    live = (k1 < kreal)
    nn1 = jnp.arange(n1 // 2, dtype=jnp.int32)
    ph = ((k1[:, None] * nn1[None, :]) % n1).astype(F32) * (2.0 * math.pi / n1)
    lv = live[:, None].astype(F32)
    f1 = jnp.concatenate([jnp.cos(ph) * lv, -jnp.sin(ph) * lv], axis=0).astype(BF16)
    kk2 = jnp.arange(n2, dtype=jnp.int32)
    nn2 = jnp.arange(n2, dtype=jnp.int32)
    num = (k1[:, None, None] * nn2[None, None, :] + n1 * kk2[None, :, None] * nn2[None, None, :]) % n
    ang = num.astype(F32) * (2.0 * math.pi / n)
    mr, mi = jnp.cos(ang), -jnp.sin(ang)
    wf = jnp.concatenate([jnp.concatenate([mr, -mi], axis=2),
                          jnp.concatenate([mi, mr], axis=2)], axis=1).astype(BF16)
    gr, gi = jnp.swapaxes(mr, 1, 2), -jnp.swapaxes(mi, 1, 2)
    wi = jnp.concatenate([jnp.concatenate([gr, -gi], axis=2),
                          jnp.concatenate([gi, gr], axis=2)], axis=1).astype(BF16)
    ck = jnp.where((k1 == 0) | (k1 == n1 // 2), 1.0, 2.0) * live.astype(F32) / n
    ph3 = ((nn1[:, None] * k1[None, :]) % n1).astype(F32) * (2.0 * math.pi / n1)
    f3 = jnp.concatenate([jnp.cos(ph3) * ck[None, :], -jnp.sin(ph3) * ck[None, :]], axis=1).astype(BF16)
    return n1, n2, kh, f1, wf, wi, f3


def _hyena_filter_inputs(seq):
    t = jnp.linspace(0.0, 1.0, seq, dtype=F32)[:, None]
    w = 2.0 * math.pi * jnp.arange(seq, dtype=F32)[:, None] / seq
    f = jnp.linspace(1e-4, HY_BANDS - 1, HY_BANDS, dtype=F32)[None, :]
    z = jnp.concatenate([t, jnp.cos(f * w), -jnp.sin(f * w)], axis=-1)
    return jnp.pad(z, ((0, 0), (0, HY_FILTER_FF - z.shape[1])))


def _hyena_mixer(x, g, sh, sc, gate, w_in, b_in, short_w, short_b, fw1, fb1, fw2, fb2, fw3, fb3,
                 sin_freq, fw4, filter_bias, w_out, b_out):
    seq, d = x.shape
    n1, n2, kh, f1, wf, wi, f3 = _dft_constants(seq)
    kb = 8
    v, g1, g2 = _hy_in(x, g, sh, sc, w_in.astype(BF16), b_in.reshape(1, -1), short_w, short_b.reshape(1, -1),
                       tm=256)

    z = _hyena_filter_inputs(seq)
    fw1p = jnp.pad(fw1, ((0, HY_FILTER_FF - fw1.shape[0]), (0, 0)))
    deltas = jnp.linspace(math.log(HY_DECAY_TARGET) / HY_SLOW_DECAY,
                          math.log(HY_DECAY_TARGET) / HY_FAST_DECAY, d, dtype=F32)[None, :]
    hf, nrm = _hy_filter(z, fw1p, fb1.reshape(1, -1), fw2, fb2.reshape(1, -1), fw3, fb3.reshape(1, -1),
                         sin_freq.reshape(1, -1), fw4.astype(BF16), deltas, tl=512)
    nrm = nrm.reshape(2, 2, d)
    inv_norm = (1.0 / (nrm[:, 0] + nrm[:, 1])).reshape(2, 1, d)
    half = n1 // 2
    fa = _dft_a(f1, hf.reshape(half, n2 * 4 * d), tn=4096)
    kspec = _filter_spectrum(wf, fa.reshape(2, kh, n2, 4 * d), inv_norm, kb)

    def conv(u, gt, order):
        a = _dft_a(f1, u.reshape(half, n2 * d), tn=4096)
        b = _dft_mid(wf, wi, a.reshape(2, kh, n2, d), kspec, order, kb)
        tn = 2 * d
        bias_t = jnp.tile(filter_bias[order].reshape(1, d), (1, tn // d))
        zz = _dft_c(f3, b.reshape(2 * kh, n2 * d), u.reshape(half, n2 * d), gt.reshape(half, n2 * d), bias_t, tn)
        return zz.reshape(seq, d)

    z1 = conv(v, g1, 0)
    z2 = conv(z1, g2, 1)
    return _resid_proj1(z2, w_out.astype(BF16), b_out.reshape(1, -1), x, gate, tm=512)


def _attention_mixer(x, ctx, g, sh, sc, sh_c, sc_c, gate, w_in, q_norm_g, kv_norm_g, w_uq, w_ukv, sink, w_out):
    s, d = x.shape
    n_ctx = ctx.shape[0]
    w_aug, wq, wkv = _attn_weights(w_in, w_uq, w_ukv)
    cos_m, sin_m = _rope_tables(s, n_ctx, MLA_ROPE)
    tot = s + n_ctx
    cm = jnp.concatenate([jnp.ones((tot, MLA_NOPE), F32), cos_m, jnp.zeros((tot, LANES - 96), F32)], axis=1)
    sm = jnp.concatenate([jnp.zeros((tot, MLA_NOPE), F32), sin_m, jnp.zeros((tot, LANES - 96), F32)], axis=1)
    cos_g, sin_g = _rope_tables(s, n_ctx, GQA_HEAD_DIM)
    cg = jnp.concatenate([cos_g, cos_g], axis=1)
    sg = jnp.concatenate([sin_g, sin_g], axis=1)
    sh2 = jnp.stack([sh_c, sh], axis=0)
    sc2 = jnp.stack([sc_c, sc], axis=0)
    mq, mk, mv, gq, gk, gv = _attn_prep(x, ctx, g, sh2, sc2, w_aug, q_norm_g.reshape(1, -1),
                                        kv_norm_g.reshape(1, -1), wq, wkv, cm, sm, cg, sg)
    mla_out = _mla_attention(mq, mk, mv, tq=min(1024, s), tk=1280 if tot % 1280 == 0 else n_ctx)
    gqa_out = _gqa_attention(sink, gq, gk, gv, n_ctx)
    wo = w_out.astype(BF16)
    nm = MLA_HEADS * MLA_V
    return _resid_proj2(mla_out, gqa_out, wo[:nm], wo[nm:], x, gate, tm=512)


def kernel(x, c, ctx, c_ctx, mod_w, mod_b, norm1_g, norm2_g, attn_w_in, mla_q_norm_g, mla_kv_norm_g,
           mla_w_uq, mla_w_ukv, gqa_sink, attn_w_out, hy_w_in, hy_b_in, hy_short_w, hy_short_b,
           hy_ffn_w1, hy_ffn_b1, hy_ffn_w2, hy_ffn_b2, hy_ffn_w3, hy_ffn_b3, hy_sin_freq, hy_ffn_w4,
           hy_filter_bias, hy_w_out, hy_b_out, moe_router_w, moe_router_b, moe_w1, moe_b1, moe_w2,
           moe_b2, final_norm_g):
    batch, seq, d = x.shape
    assert batch == 1
    depth = mod_w.shape[0]
    xs = x.reshape(seq, d)
    cs = ctx.reshape(-1, d)
    mod = _modulation(c, c_ctx, mod_w, mod_b)
    fg = final_norm_g.reshape(1, d)
    for i in range(depth):
        m = mod[i, 0].reshape(6, 1, d)
        mc = mod[i, 1].reshape(6, 1, d)
        g1 = norm1_g[i].reshape(1, d)
        j = i // 2
        if i % 2 == 0:
            xs = _attention_mixer(xs, cs, g1, m[0], m[1], mc[0], mc[1], m[2], attn_w_in[j], mla_q_norm_g[j],
                                  mla_kv_norm_g[j], mla_w_uq[j], mla_w_ukv[j], gqa_sink[j], attn_w_out[j])
        else:
            xs = _hyena_mixer(xs, g1, m[0], m[1], m[2], hy_w_in[j], hy_b_in[j], hy_short_w[j], hy_short_b[j],
                              hy_ffn_w1[j], hy_ffn_b1[j], hy_ffn_w2[j], hy_ffn_b2[j], hy_ffn_w3[j], hy_ffn_b3[j],
                              hy_sin_freq[j], hy_ffn_w4[j], hy_filter_bias[j], hy_w_out[j], hy_b_out[j])
        xs = _moe_layer(xs, norm2_g[i].reshape(1, d), m[3], m[4], m[5], moe_router_w[i], moe_router_b[i],
                        moe_w1, moe_b1, moe_w2, moe_b2, fg, final_norm=(i == depth - 1), layer=i)
    return xs.reshape(batch, seq, d)
```

```python
import functools
import math

import jax
import jax.numpy as jnp
from jax import lax
from jax.experimental import pallas as pl
from jax.experimental.pallas import tpu as pltpu

F32 = jnp.float32
BF16 = jnp.bfloat16

RMS_EPS = 1e-6
ROPE_BASE = 10000.0
GRID_W = 64
NEG_INF = -1e30

MLA_HEADS = 8
MLA_Q_RANK = 256
MLA_KV_RANK = 128
MLA_NOPE = 64
MLA_ROPE = 32
MLA_V = 64
MLA_SCALE = 1.0 / math.sqrt(MLA_NOPE + MLA_ROPE)
LOG2_E = math.log2(math.e)

GQA_Q_HEADS = 8
GQA_KV_HEADS = 2
GQA_HEAD_DIM = 64
GQA_SCALE = 1.0 / math.sqrt(GQA_HEAD_DIM)
WIN_BLOCK = 128

OFF_CKV = 0
OFF_KROPE = OFF_CKV + MLA_KV_RANK
OFF_GK = OFF_KROPE + MLA_ROPE
OFF_GV = OFF_GK + GQA_KV_HEADS * GQA_HEAD_DIM
KV_COLS = OFF_GV + GQA_KV_HEADS * GQA_HEAD_DIM
OFF_CQ = KV_COLS
OFF_GQ = OFF_CQ + MLA_Q_RANK
PROJ_COLS = OFF_GQ + GQA_Q_HEADS * GQA_HEAD_DIM

HY_BANDS = 16
HY_FILTER_FF = 64
HY_DECAY_TARGET = 1e-2
HY_FAST_DECAY = 0.3
HY_SLOW_DECAY = 1.5

N_EXPERTS = 32
TOP_K = 4
SWIGLU_LIMIT = 7.0
SWIGLU_ALPHA = 1.702

LANES = 128
VMEM_LIMIT_BYTES = 56 * 1024 * 1024

C_CKV = 0
C_CQ = C_CKV + 128
C_GQ = C_CQ + 256
C_GQR = C_GQ + 512
C_GK = C_GQR + 512
C_GKR = C_GK + 512
C_GV = C_GKR + 512
C_KR = C_GV + 256
C_KRR = C_KR + 128
AUG_COLS = C_KRR + 128


def _params(*sem):
    return pltpu.CompilerParams(dimension_semantics=sem, vmem_limit_bytes=VMEM_LIMIT_BYTES)


def _const_spec(shape):
    n = len(shape)
    return pl.BlockSpec(shape, lambda *_: (0,) * n)


def _norm_mod(x, g, sh, sc):
    ms = jnp.mean(x * x, axis=-1, keepdims=True)
    return (x * lax.rsqrt(ms + RMS_EPS) * g) * (1.0 + sc) + sh


def _rms(x, g):
    ms = jnp.mean(x * x, axis=-1, keepdims=True)
    return x * lax.rsqrt(ms + RMS_EPS) * g


def _mod_kernel(cc_ref, w_ref, b_ref, o_ref):
    cc = cc_ref[...]
    s = cc / (1.0 + jnp.exp(-cc))
    w = w_ref[...]
    b = b_ref[...]
    o_ref[0:1, :] = jnp.sum(w * s[:, 0:1], axis=0, keepdims=True) + b
    o_ref[1:2, :] = jnp.sum(w * s[:, 1:2], axis=0, keepdims=True) + b


def _modulation(c, c_ctx, mod_w, mod_b):
    depth, d, n = mod_w.shape
    tn = 1536
    cc = jnp.stack([c.reshape(d), c_ctx.reshape(d)], axis=1)
    return pl.pallas_call(
        _mod_kernel,
        grid=(depth, n // tn),
        in_specs=[
            _const_spec((d, 2)),
            pl.BlockSpec((None, d, tn), lambda l, j: (l, 0, j)),
            pl.BlockSpec((None, 1, tn), lambda l, j: (l, 0, j)),
        ],
        out_specs=pl.BlockSpec((None, 2, tn), lambda l, j: (l, 0, j)),
        out_shape=jax.ShapeDtypeStruct((depth, 2, n), F32),
        compiler_params=_params("parallel", "parallel"),
        name="modulation",
    )(cc, mod_w, mod_b.reshape(depth, 1, n))


def _rot_cols_1d(w):
    half = w.shape[-1] // 2
    return jnp.concatenate([-w[..., half:], w[..., :half]], axis=-1)


def _rot_cols_2d(w):
    half = w.shape[-1] // 2
    return jnp.concatenate([_rot_cols_1d(w[..., :half]), _rot_cols_1d(w[..., half:])], axis=-1)


def _rope_tables(n_lat, n_ctx, dims):
    t = jnp.arange(n_lat, dtype=jnp.int32)
    row = (t // GRID_W).astype(F32)
    col = (t % GRID_W).astype(F32)
    q = dims // 4
    inv = ROPE_BASE ** (-jnp.arange(q, dtype=F32) / q)
    ar = row[:, None] * inv[None, :]
    ac = col[:, None] * inv[None, :]
    cos = jnp.concatenate([jnp.cos(ar), jnp.cos(ar), jnp.cos(ac), jnp.cos(ac)], axis=-1)
    sin = jnp.concatenate([jnp.sin(ar), jnp.sin(ar), jnp.sin(ac), jnp.sin(ac)], axis=-1)
    cos = jnp.concatenate([jnp.ones((n_ctx, dims), F32), cos], axis=0)
    sin = jnp.concatenate([jnp.zeros((n_ctx, dims), F32), sin], axis=0)
    return cos, sin


def _attn_weights(w_in, w_uq, w_ukv):
    d = w_in.shape[0]
    z64 = jnp.zeros((d, 64), F32)
    z32 = jnp.zeros((d, 32), F32)
    ckv = w_in[:, OFF_CKV:OFF_KROPE]
    krope = w_in[:, OFF_KROPE:OFF_GK]
    gk = w_in[:, OFF_GK:OFF_GV]
    gv = w_in[:, OFF_GV:KV_COLS]
    cq = w_in[:, OFF_CQ:OFF_GQ]
    gq = w_in[:, OFF_GQ:PROJ_COLS]
    gq_rot = _rot_cols_2d(gq.reshape(d, GQA_Q_HEADS, GQA_HEAD_DIM)).reshape(d, -1)
    k0, k1 = gk[:, :64], gk[:, 64:]
    k0r, k1r = _rot_cols_2d(k0), _rot_cols_2d(k1)
    gk4 = jnp.concatenate([k0, z64, z64, k0, k1, z64, z64, k1], axis=1)
    gk4r = jnp.concatenate([k0r, z64, z64, k0r, k1r, z64, z64, k1r], axis=1)
    v0, v1 = gv[:, :64], gv[:, 64:]
    gv2 = jnp.concatenate([v0, v0, v1, v1], axis=1)
    kr = jnp.concatenate([z64, krope, z32], axis=1)
    krr = jnp.concatenate([z64, _rot_cols_2d(krope), z32], axis=1)
    w_aug = jnp.concatenate([ckv, cq, gq, gq_rot, gk4, gk4r, gv2, kr, krr], axis=1).astype(BF16)

    r = w_uq.shape[0]
    q3 = w_uq.reshape(r, MLA_HEADS, MLA_NOPE + MLA_ROPE)
    q_main = jnp.pad(q3, ((0, 0), (0, 0), (0, LANES - MLA_NOPE - MLA_ROPE)))
    q_rot = jnp.concatenate([jnp.zeros((r, MLA_HEADS, MLA_NOPE), F32), _rot_cols_2d(q3[..., MLA_NOPE:]),
                             jnp.zeros((r, MLA_HEADS, LANES - MLA_NOPE - MLA_ROPE), F32)], axis=-1)
    wq = jnp.concatenate([q_main.reshape(r, -1), q_rot.reshape(r, -1)], axis=1).astype(BF16)

    rk = w_ukv.shape[0]
    kv3 = w_ukv.reshape(rk, MLA_HEADS, MLA_NOPE + MLA_V)
    k_part = jnp.pad(kv3[..., :MLA_NOPE], ((0, 0), (0, 0), (0, LANES - MLA_NOPE))).reshape(rk, -1)
    v_part = kv3[..., MLA_NOPE:].reshape(rk, -1)
    wkv = jnp.concatenate([k_part, v_part], axis=1).astype(BF16)
    return w_aug, wq, wkv


def _attn_prep_kernel(x_ref, ctx_ref, g_ref, sh_ref, sc_ref, w_ref, gq_ref, gkv_ref, wq_ref, wkv_ref,
                      cm_ref, sm_ref, cg_ref, sg_ref,
                      mq_ref, mk_ref, mv_ref, gqo_ref, gko_ref, gvo_ref):
    i = pl.program_id(0)
    xin = jnp.where(i == 0, ctx_ref[...], x_ref[...])
    h = _norm_mod(xin, g_ref[...], sh_ref[...], sc_ref[...])
    p = jnp.dot(h.astype(BF16), w_ref[...], preferred_element_type=F32)

    ckv = _rms(p[:, C_CKV:C_CKV + 128], gkv_ref[...])
    kv = jnp.dot(ckv.astype(BF16), wkv_ref[...], preferred_element_type=F32)
    cq = _rms(p[:, C_CQ:C_CQ + 256], gq_ref[...])
    qa = jnp.dot(cq.astype(BF16), wq_ref[...], preferred_element_type=F32)

    cm, sm = cm_ref[...], sm_ref[...]
    kr = p[:, C_KR:C_KR + 128] * cm + p[:, C_KRR:C_KRR + 128] * sm
    for hd in range(MLA_HEADS):
        lo = hd * LANES
        qh = qa[:, lo:lo + LANES] * cm + qa[:, 1024 + lo:1024 + lo + LANES] * sm
        mq_ref[hd] = (qh * (MLA_SCALE * LOG2_E)).astype(BF16)
        mk_ref[hd] = (kv[:, lo:lo + LANES] + kr).astype(BF16)
    mv_ref[...] = kv[:, 1024:1536].astype(BF16)

    cg, sg = cg_ref[...], sg_ref[...]
    for gI in range(4):
        lo = gI * LANES
        gq = p[:, C_GQ + lo:C_GQ + lo + LANES] * cg + p[:, C_GQR + lo:C_GQR + lo + LANES] * sg
        gqo_ref[:, lo:lo + LANES] = (gq * GQA_SCALE).astype(BF16)
        gk = p[:, C_GK + lo:C_GK + lo + LANES] * cg + p[:, C_GKR + lo:C_GKR + lo + LANES] * sg
        gko_ref[:, lo:lo + LANES] = gk.astype(BF16)
    gvo_ref[...] = p[:, C_GV:C_GV + 256].astype(BF16)


def _attn_prep(x, ctx, g, sh2, sc2, w_aug, gq, gkv, wq, wkv, cm, sm, cg, sg):
    s, d = x.shape
    c = ctx.shape[0]
    tm = c
    nt = s // tm + 1
    tot = s + c
    lat = lambda i: (jnp.maximum(i - 1, 0), 0)
    row = lambda i: (i, 0)
    return pl.pallas_call(
        _attn_prep_kernel,
        grid=(nt,),
        in_specs=[
            pl.BlockSpec((tm, d), lat),
            _const_spec((c, d)),
            _const_spec((1, d)),
            pl.BlockSpec((None, 1, d), lambda i: (jnp.minimum(i, 1), 0, 0)),
            pl.BlockSpec((None, 1, d), lambda i: (jnp.minimum(i, 1), 0, 0)),
            _const_spec(w_aug.shape),
            _const_spec(gq.shape),
            _const_spec(gkv.shape),
            _const_spec(wq.shape),
            _const_spec(wkv.shape),
            pl.BlockSpec((tm, LANES), row),
            pl.BlockSpec((tm, LANES), row),
            pl.BlockSpec((tm, LANES), row),
            pl.BlockSpec((tm, LANES), row),
        ],
        out_specs=[
            pl.BlockSpec((MLA_HEADS, tm, LANES), lambda i: (0, jnp.maximum(i - 1, 0), 0)),
            pl.BlockSpec((MLA_HEADS, tm, LANES), lambda i: (0, i, 0)),
            pl.BlockSpec((tm, 512), row),
            pl.BlockSpec((tm, 512), lat),
            pl.BlockSpec((tm, 512), row),
            pl.BlockSpec((tm, 256), row),
        ],
        out_shape=[
            jax.ShapeDtypeStruct((MLA_HEADS, s, LANES), BF16),
            jax.ShapeDtypeStruct((MLA_HEADS, tot, LANES), BF16),
            jax.ShapeDtypeStruct((tot, 512), BF16),
            jax.ShapeDtypeStruct((s, 512), BF16),
            jax.ShapeDtypeStruct((tot, 512), BF16),
            jax.ShapeDtypeStruct((tot, 256), BF16),
        ],
        compiler_params=_params("arbitrary"),
        name="attn_prep",
    )(x, ctx, g, sh2, sc2, w_aug, gq, gkv, wq, wkv, cm, sm, cg, sg)


MLA_Q_SUB = 128


def _mla_kernel(q_ref, k_ref, v_ref, o_ref, m_sc, acc_sc):
    j = pl.program_id(2)

    @pl.when(j == 0)
    def _():
        m_sc[...] = jnp.full(m_sc.shape, -jnp.inf, F32)
        acc_sc[...] = jnp.zeros(acc_sc.shape, F32)

    v = v_ref[...]
    v1 = jnp.concatenate([v, jnp.ones(v.shape, BF16)], axis=1)
    tq = q_ref.shape[1]
    sub = min(MLA_Q_SUB, tq)
    for hh in range(2):
        k = k_ref[hh]
        for r0 in range(0, tq, sub):
            rows = pl.ds(r0, sub)
            s = lax.dot_general(q_ref[hh, rows, :], k, (((1,), (1,)), ((), ())),
                                preferred_element_type=F32)
            m_prev = m_sc[hh, rows, :]
            m_new = jnp.maximum(m_prev, jnp.max(s, axis=-1, keepdims=True))
            alpha = jnp.exp2(m_prev - m_new)
            p = jnp.exp2(s - m_new).astype(BF16)
            acc_sc[hh, rows, :] = alpha * acc_sc[hh, rows, :] + jnp.dot(p, v1, preferred_element_type=F32)
            m_sc[hh, rows, :] = m_new

    @pl.when(j == pl.num_programs(2) - 1)
    def _():
        lane = lax.broadcasted_iota(jnp.int32, (acc_sc.shape[1], LANES), 1)
        o0 = acc_sc[0, :, :LANES] / acc_sc[0, :, LANES:]
        o1 = acc_sc[1, :, :LANES] / acc_sc[1, :, LANES:]
        o_ref[...] = jnp.where(lane < MLA_V, o0, o1).astype(o_ref.dtype)


def _mla_attention(mq, mk, mv, tq, tk):
    _, s, _ = mq.shape
    tot = mk.shape[1]
    return pl.pallas_call(
        _mla_kernel,
        grid=(MLA_HEADS // 2, s // tq, tot // tk),
        in_specs=[
            pl.BlockSpec((2, tq, LANES), lambda h, i, j: (h, i, 0)),
            pl.BlockSpec((2, tk, LANES), lambda h, i, j: (h, j, 0)),
            pl.BlockSpec((tk, LANES), lambda h, i, j: (j, h)),
        ],
        out_specs=pl.BlockSpec((tq, LANES), lambda h, i, j: (i, h)),
        out_shape=jax.ShapeDtypeStruct((s, MLA_HEADS * MLA_V), BF16),
        scratch_shapes=[
            pltpu.VMEM((2, tq, 1), F32),
            pltpu.VMEM((2, tq, 2 * LANES), F32),
        ],
        compiler_params=_params("parallel", "parallel", "arbitrary"),
        name="mla_flash",
    )(mq, mk, mv)


def _gqa_kernel(sink_ref, q_ref, kp_ref, ko_ref, kn_ref, kc_ref, vp_ref, vo_ref, vn_ref, vc_ref, o_ref):
    n = pl.program_id(0)
    nb = pl.num_programs(0)
    wb = WIN_BLOCK
    r = lax.broadcasted_iota(jnp.int32, (wb, wb), 0)
    c = lax.broadcasted_iota(jnp.int32, (wb, wb), 1)
    ok_prev = c >= r + jnp.where(n >= 1, 0, wb)
    ok_next = c + jnp.where(n <= nb - 2, 0, wb) <= r
    lane = lax.broadcasted_iota(jnp.int32, (wb, LANES), 1)
    dn = (((1,), (1,)), ((), ()))
    for pr in range(GQA_Q_HEADS // 2):
        kvh = pr // 2
        q = q_ref[:, pr * LANES:(pr + 1) * LANES]
        vlo = kvh * LANES
        outs = []
        for var in range(2):
            hd = 2 * pr + var
            klo = (2 * kvh + var) * LANES
            s_p = lax.dot_general(q, kp_ref[:, klo:klo + LANES], dn, preferred_element_type=F32)
            s_o = lax.dot_general(q, ko_ref[:, klo:klo + LANES], dn, preferred_element_type=F32)
            s_n = lax.dot_general(q, kn_ref[:, klo:klo + LANES], dn, preferred_element_type=F32)
            s_c = lax.dot_general(q, kc_ref[:, klo:klo + LANES], dn, preferred_element_type=F32)
            s_p = jnp.where(ok_prev, s_p, NEG_INF)
            s_n = jnp.where(ok_next, s_n, NEG_INF)
            sink = sink_ref[hd]
            m = jnp.maximum(jnp.maximum(jnp.max(s_p, axis=-1, keepdims=True),
                                        jnp.max(s_o, axis=-1, keepdims=True)),
                            jnp.maximum(jnp.max(s_n, axis=-1, keepdims=True),
                                        jnp.max(s_c, axis=-1, keepdims=True)))
            m = jnp.maximum(m, sink)
            e_p, e_o, e_n, e_c = jnp.exp(s_p - m), jnp.exp(s_o - m), jnp.exp(s_n - m), jnp.exp(s_c - m)
            den = (jnp.sum(e_p, axis=-1, keepdims=True) + jnp.sum(e_o, axis=-1, keepdims=True)
                   + jnp.sum(e_n, axis=-1, keepdims=True) + jnp.sum(e_c, axis=-1, keepdims=True)
                   + jnp.exp(sink - m))
            o = (jnp.dot(e_p.astype(BF16), vp_ref[:, vlo:vlo + LANES], preferred_element_type=F32)
                 + jnp.dot(e_o.astype(BF16), vo_ref[:, vlo:vlo + LANES], preferred_element_type=F32)
                 + jnp.dot(e_n.astype(BF16), vn_ref[:, vlo:vlo + LANES], preferred_element_type=F32)
                 + jnp.dot(e_c.astype(BF16), vc_ref[:, vlo:vlo + LANES], preferred_element_type=F32))
            outs.append(o / den)
        o_ref[:, pr * LANES:(pr + 1) * LANES] = jnp.where(lane < GQA_HEAD_DIM, outs[0], outs[1]).astype(o_ref.dtype)


def _gqa_attention(sink, gq, gk, gv, n_ctx):
    s = gq.shape[0]
    wb = WIN_BLOCK
    nb = s // wb
    cb = n_ctx // wb
    prev = lambda n: (n + cb - 1, 0)
    own = lambda n: (n + cb, 0)
    nxt = lambda n: (jnp.minimum(n + cb + 1, nb + cb - 1), 0)
    return pl.pallas_call(
        _gqa_kernel,
        grid=(nb,),
        in_specs=[
            pl.BlockSpec(memory_space=pltpu.SMEM),
            pl.BlockSpec((wb, 512), lambda n: (n, 0)),
            pl.BlockSpec((wb, 512), prev),
            pl.BlockSpec((wb, 512), own),
            pl.BlockSpec((wb, 512), nxt),
            _const_spec((n_ctx, 512)),
            pl.BlockSpec((wb, 256), prev),
            pl.BlockSpec((wb, 256), own),
            pl.BlockSpec((wb, 256), nxt),
            _const_spec((n_ctx, 256)),
        ],
        out_specs=pl.BlockSpec((wb, 512), lambda n: (n, 0)),
        out_shape=jax.ShapeDtypeStruct((s, 512), BF16),
        compiler_params=_params("parallel"),
        name="gqa_window",
    )(sink, gq, gk, gk, gk, gk, gv, gv, gv, gv)


def _resid_proj2_kernel(a_ref, b_ref, wa_ref, wb_ref, x_ref, gt_ref, o_ref):
    y = (jnp.dot(a_ref[...], wa_ref[...], preferred_element_type=F32)
         + jnp.dot(b_ref[...], wb_ref[...], preferred_element_type=F32))
    o_ref[...] = x_ref[...] + gt_ref[...] * y


def _resid_proj2(a, b, wa, wb, x, gate, tm):
    s, d = x.shape
    row = lambda i: (i, 0)
    return pl.pallas_call(
        _resid_proj2_kernel,
        grid=(s // tm,),
        in_specs=[pl.BlockSpec((tm, a.shape[1]), row), pl.BlockSpec((tm, b.shape[1]), row),
                  _const_spec(wa.shape), _const_spec(wb.shape),
                  pl.BlockSpec((tm, d), row), _const_spec((1, d))],
        out_specs=pl.BlockSpec((tm, d), row),
        out_shape=jax.ShapeDtypeStruct((s, d), F32),
        compiler_params=_params("parallel"),
        name="attn_out_proj",
    )(a, b, wa, wb, x, gate)


def _resid_proj1_kernel(a_ref, w_ref, bias_ref, x_ref, gt_ref, o_ref):
    y = jnp.dot(a_ref[...], w_ref[...], preferred_element_type=F32) + bias_ref[...]
    o_ref[...] = x_ref[...] + gt_ref[...] * y


def _resid_proj1(a, w, bias, x, gate, tm):
    s, d = x.shape
    row = lambda i: (i, 0)
    return pl.pallas_call(
        _resid_proj1_kernel,
        grid=(s // tm,),
        in_specs=[pl.BlockSpec((tm, a.shape[1]), row), _const_spec(w.shape), _const_spec((1, d)),
                  pl.BlockSpec((tm, d), row), _const_spec((1, d))],
        out_specs=pl.BlockSpec((tm, d), row),
        out_shape=jax.ShapeDtypeStruct((s, d), F32),
        compiler_params=_params("parallel"),
        name="hyena_out_proj",
    )(a, w, bias, x, gate)


META_IDX, META_W, META_RANK = 0, 4, 8


def _router_kernel(x_ref, g_ref, sh_ref, sc_ref, rw_ref, rb_ref, t_ref, meta_ref, cnt_ref, carry_sc):
    i = pl.program_id(0)

    @pl.when(i == 0)
    def _():
        carry_sc[...] = jnp.zeros(carry_sc.shape, F32)

    t = _norm_mod(x_ref[...], g_ref[...], sh_ref[...], sc_ref[...])
    t_ref[...] = t
    logits = jnp.dot(t, rw_ref[...], preferred_element_type=F32,
                     precision=lax.Precision.HIGHEST) + rb_ref[...]
    tm = logits.shape[0]
    lane = lax.broadcasted_iota(jnp.int32, logits.shape, 1)
    work = logits
    vals, hots, idxs = [], [], []
    for _ in range(TOP_K):
        m = jnp.max(work, axis=-1, keepdims=True)
        idx = jnp.min(jnp.where(work == m, lane, N_EXPERTS), axis=-1, keepdims=True)
        hot = lane == idx
        vals.append(m)
        hots.append(hot)
        idxs.append(idx)
        work = jnp.where(hot, -jnp.inf, work)
    es = [jnp.exp(v - vals[0]) for v in vals]
    den = es[0] + es[1] + es[2] + es[3]
    hot_all = jnp.zeros(logits.shape, F32)
    for hot in hots:
        hot_all = hot_all + jnp.where(hot, 1.0, 0.0)
    r = lax.broadcasted_iota(jnp.int32, (tm, tm), 0)
    c = lax.broadcasted_iota(jnp.int32, (tm, tm), 1)
    tri = jnp.where(r > c, 1.0, 0.0).astype(BF16)
    rank = jnp.dot(tri, hot_all.astype(BF16), preferred_element_type=F32) + carry_sc[...]
    carry_sc[...] += jnp.sum(hot_all, axis=0, keepdims=True)
    cnt_ref[...] = carry_sc[...]
    lane_m = lax.broadcasted_iota(jnp.int32, (tm, LANES), 1)
    meta = jnp.zeros((tm, LANES), F32)
    for k in range(TOP_K):
        rk = jnp.sum(jnp.where(hots[k], rank, 0.0), axis=-1, keepdims=True)
        meta = jnp.where(lane_m == META_IDX + k, idxs[k].astype(F32), meta)
        meta = jnp.where(lane_m == META_W + k, es[k] / den, meta)
        meta = jnp.where(lane_m == META_RANK + k, rk, meta)
    meta_ref[...] = meta


def _router(x, g, sh, sc, rw, rb, tm):
    s, d = x.shape
    e = rw.shape[1]
    row = lambda i: (i, 0)
    return pl.pallas_call(
        _router_kernel,
        grid=(s // tm,),
        in_specs=[pl.BlockSpec((tm, d), row), _const_spec((1, d)), _const_spec((1, d)), _const_spec((1, d)),
                  _const_spec((d, e)), _const_spec((1, e))],
        out_specs=[pl.BlockSpec((tm, d), row), pl.BlockSpec((tm, LANES), row), _const_spec((1, e))],
        out_shape=[jax.ShapeDtypeStruct((s, d), F32), jax.ShapeDtypeStruct((s, LANES), F32),
                   jax.ShapeDtypeStruct((1, e), F32)],
        scratch_shapes=[pltpu.VMEM((1, e), F32)],
        compiler_params=_params("arbitrary"),
        name="moe_router",
    )(x, g, sh, sc, rw, rb)


def _rows_copy(src, dst, sem, n):
    return pltpu.make_async_copy(src.at[pl.ds(0, n)], dst.at[pl.ds(0, n)], sem)


def _dispatch_kernel(pos_ref, t_ref, xs_init_ref, xs_ref, sem, *, tm):
    del xs_init_ref
    base = pl.program_id(0) * tm

    def body(r, carry):
        for k in range(TOP_K):
            p = pos_ref[(base + r) * TOP_K + k]
            pltpu.make_async_copy(t_ref.at[pl.ds(r, 1)], xs_ref.at[pl.ds(p, 1)], sem).start(priority=k % 2)
        return carry

    lax.fori_loop(0, tm, body, 0, unroll=8)
    for k in range(TOP_K):
        _rows_copy(t_ref, xs_ref, sem, tm).wait()


def _dispatch(pos, t, rows, tm):
    s, d = t.shape
    grid_spec = pltpu.PrefetchScalarGridSpec(
        num_scalar_prefetch=1,
        grid=(s // tm,),
        in_specs=[pl.BlockSpec((tm, d), lambda i, pos: (i, 0)),
                  pl.BlockSpec(memory_space=pl.ANY)],
        out_specs=pl.BlockSpec(memory_space=pl.ANY),
        scratch_shapes=[pltpu.SemaphoreType.DMA(())],
    )
    return pl.pallas_call(
        functools.partial(_dispatch_kernel, tm=tm),
        grid_spec=grid_spec,
        out_shape=jax.ShapeDtypeStruct((rows, d), F32),
        input_output_aliases={2: 0},
        compiler_params=_params("arbitrary"),
        name="moe_dispatch",
    )(pos, t, jnp.zeros((rows, d), F32))


def _moe_expert_kernel(te_ref, nu_ref, xs_ref, w1_ref, b1_ref, w2_ref, b2_ref, y_ref, w1b, w2b, *, ff):
    i = pl.program_id(0)
    e = te_ref[i]
    e_prev = te_ref[jnp.maximum(i - 1, 0)]

    @pl.when((i == 0) | (e != e_prev))
    def _():
        w1b[...] = w1_ref[...].astype(BF16)
        w2b[...] = w2_ref[...].astype(BF16)

    @pl.when(i < nu_ref[0])
    def _():
        u = jnp.dot(xs_ref[...].astype(BF16), w1b[...], preferred_element_type=F32) + b1_ref[...]
        glu = jnp.minimum(u[:, :ff], SWIGLU_LIMIT)
        lin = jnp.clip(u[:, ff:], -SWIGLU_LIMIT, SWIGLU_LIMIT)
        act = glu / (1.0 + jnp.exp(-SWIGLU_ALPHA * glu)) * (lin + 1.0)
        y_ref[...] = jnp.dot(act.astype(BF16), w2b[...], preferred_element_type=F32) + b2_ref[...]

    @pl.when(i >= nu_ref[0])
    def _():
        y_ref[...] = jnp.zeros(y_ref.shape, F32)


def _moe_experts(tile_expert, n_used, xs, w1, b1, w2, b2, tmm, layer):
    rows, d = xs.shape
    _, ne, _, ff2 = w1.shape
    ff = ff2 // 2
    wsel = lambda i, te, nu: (layer, te[i], 0, 0)
    grid_spec = pltpu.PrefetchScalarGridSpec(
        num_scalar_prefetch=2,
        grid=(rows // tmm,),
        in_specs=[pl.BlockSpec((tmm, d), lambda i, te, nu: (jnp.minimum(i, nu[0] - 1), 0)),
                  pl.BlockSpec((None, None, d, ff2), wsel),
                  pl.BlockSpec((None, None, 1, ff2), wsel),
                  pl.BlockSpec((None, None, ff, d), wsel),
                  pl.BlockSpec((None, None, 1, d), wsel)],
        out_specs=pl.BlockSpec((tmm, d), lambda i, te, nu: (i, 0)),
        scratch_shapes=[pltpu.VMEM((d, ff2), BF16), pltpu.VMEM((ff, d), BF16)],
    )
    return pl.pallas_call(
        functools.partial(_moe_expert_kernel, ff=ff),
        grid_spec=grid_spec,
        out_shape=jax.ShapeDtypeStruct((rows, d), F32),
        compiler_params=_params("arbitrary"),
        name="moe_experts",
    )(tile_expert, n_used, xs, w1, b1.reshape(-1, ne, 1, ff2), w2, b2.reshape(-1, ne, 1, d))


def _combine_kernel(pos_ref, y_ref, meta_ref, x_ref, gt_ref, fg_ref, o_ref, buf, sem, *, tc, final_norm):
    base = pl.program_id(0) * tc

    def body(r, carry):
        for k in range(TOP_K):
            p = pos_ref[(base + r) * TOP_K + k]
            pltpu.make_async_copy(y_ref.at[pl.ds(p, 1)], buf.at[k, pl.ds(r, 1)], sem).start(priority=k % 2)
        return carry

    lax.fori_loop(0, tc, body, 0, unroll=8)
    for k in range(TOP_K):
        _rows_copy(y_ref, buf.at[k], sem, tc).wait()
    meta = meta_ref[...]
    acc = buf[0] * meta[:, META_W:META_W + 1]
    for k in range(1, TOP_K):
        acc = acc + buf[k] * meta[:, META_W + k:META_W + k + 1]
    xo = x_ref[...] + gt_ref[...] * acc
    if final_norm:
        xo = _rms(xo, fg_ref[...])
    o_ref[...] = xo


def _combine(pos, y, meta, x, gate, final_g, tc, final_norm):
    s, d = x.shape
    row = lambda i, pos: (i, 0)
    grid_spec = pltpu.PrefetchScalarGridSpec(
        num_scalar_prefetch=1,
        grid=(s // tc,),
        in_specs=[pl.BlockSpec(memory_space=pl.ANY),
                  pl.BlockSpec((tc, LANES), row),
                  pl.BlockSpec((tc, d), row),
                  pl.BlockSpec((1, d), lambda i, pos: (0, 0)),
                  pl.BlockSpec((1, d), lambda i, pos: (0, 0))],
        out_specs=pl.BlockSpec((tc, d), row),
        scratch_shapes=[pltpu.VMEM((TOP_K, tc, d), F32), pltpu.SemaphoreType.DMA(())],
    )
    return pl.pallas_call(
        functools.partial(_combine_kernel, tc=tc, final_norm=final_norm),
        grid_spec=grid_spec,
        out_shape=jax.ShapeDtypeStruct((s, d), F32),
        compiler_params=_params("arbitrary"),
        name="moe_combine",
    )(pos, y, meta, x, gate, final_g)


MOE_ROW_TILE = 256


def _moe_layer(x, g, sh, sc, gate, rw, rb, w1, b1, w2, b2, final_g, final_norm, layer):
    s, d = x.shape
    ne = rw.shape[1]
    tmm = MOE_ROW_TILE
    t, meta, cnt = _router(x, g, sh, sc, rw, rb.reshape(1, -1), tm=512)
    idx = meta[:, META_IDX:META_IDX + TOP_K].astype(jnp.int32)
    rank = meta[:, META_RANK:META_RANK + TOP_K].astype(jnp.int32)
    counts = cnt[0].astype(jnp.int32)
    padded = ((counts + tmm - 1) // tmm) * tmm
    ends = jnp.cumsum(padded)
    pos = (jnp.take(ends - padded, idx) + rank).reshape(-1)
    rows = TOP_K * s + ne * tmm
    nt = rows // tmm
    tile_ends = ends // tmm
    n_used = tile_ends[-1]
    tiles = jnp.arange(nt, dtype=jnp.int32)
    last = jnp.maximum(n_used - 1, 0)
    te = jnp.sum((jnp.minimum(tiles, last)[:, None] >= tile_ends[None, :]).astype(jnp.int32), axis=1)
    te = jnp.minimum(te, ne - 1)
    xs = _dispatch(pos, t, rows, tm=512)
    y = _moe_experts(te, n_used.reshape(1), xs, w1, b1, w2, b2, tmm, layer)
    return _combine(pos, y, meta, x, gate, final_g, tc=256, final_norm=final_norm)


def _hy_in_kernel(xp_ref, x_ref, xn_ref, g_ref, sh_ref, sc_ref, w_ref, b_ref, sw_ref, sb_ref,
                  v_ref, g1_ref, g2_ref, *, tm, seq):
    i = pl.program_id(0)
    d = x_ref.shape[1]
    xe = jnp.concatenate([xp_ref[...], x_ref[...], xn_ref[...]], axis=0)
    h = _norm_mod(xe, g_ref[...], sh_ref[...], sc_ref[...])
    p = jnp.dot(h.astype(BF16), w_ref[...], preferred_element_type=F32) + b_ref[...]
    row = i * tm - 8 + lax.broadcasted_iota(jnp.int32, (tm + 16, 1), 0)
    p = jnp.where((row >= 0) & (row < seq), p, 0.0)
    pm = pltpu.roll(p, 1, 0)[8:8 + tm]
    pc = p[8:8 + tm]
    pn = pltpu.roll(p, tm + 15, 0)[8:8 + tm]
    sw = sw_ref[...]
    out = sw[0:1] * pm + sw[1:2] * pc + sw[2:3] * pn + sb_ref[...]
    v_ref[...] = out[:, :d].astype(BF16)
    g1_ref[...] = out[:, d:2 * d].astype(BF16)
    g2_ref[...] = out[:, 2 * d:].astype(BF16)


def _hy_in(x, g, sh, sc, w, b, sw, sb, tm):
    s, d = x.shape
    n3 = w.shape[1]
    r8 = tm // 8
    nblk8 = s // 8
    row = lambda i: (i, 0)
    out = jax.ShapeDtypeStruct((s, d), BF16)
    return pl.pallas_call(
        functools.partial(_hy_in_kernel, tm=tm, seq=s),
        grid=(s // tm,),
        in_specs=[
            pl.BlockSpec((8, d), lambda i: (jnp.maximum(i * r8 - 1, 0), 0)),
            pl.BlockSpec((tm, d), row),
            pl.BlockSpec((8, d), lambda i: (jnp.minimum((i + 1) * r8, nblk8 - 1), 0)),
            _const_spec((1, d)), _const_spec((1, d)), _const_spec((1, d)),
            _const_spec((d, n3)), _const_spec((1, n3)), _const_spec((3, n3)), _const_spec((1, n3)),
        ],
        out_specs=[pl.BlockSpec((tm, d), row)] * 3,
        out_shape=[out, out, out],
        compiler_params=_params("parallel"),
        name="hyena_in_proj",
    )(x, x, x, g, sh, sc, w, b, sw, sb)


def _hy_filter_kernel(z_ref, w1_ref, b1_ref, w2_ref, b2_ref, w3_ref, b3_ref, fr_ref, w4_ref, dl_ref,
                      hf_ref, nrm_ref):
    i = pl.program_id(0)
    hp = lax.Precision.HIGHEST
    z = z_ref[...]
    fr = fr_ref[...]
    a = jnp.sin(fr * (jnp.dot(z, w1_ref[...], preferred_element_type=F32, precision=hp) + b1_ref[...]))
    a = jnp.sin(fr * (jnp.dot(a, w2_ref[...], preferred_element_type=F32, precision=hp) + b2_ref[...]))
    a = jnp.sin(fr * (jnp.dot(a, w3_ref[...], preferred_element_type=F32, precision=hp) + b3_ref[...]))
    hf = jnp.dot(a.astype(BF16), w4_ref[...], preferred_element_type=F32)
    decay = jnp.exp(-z[:, 0:1] * jnp.abs(dl_ref[...]))
    d = decay.shape[1]
    cols = []
    for q in range(4):
        cols.append(hf[:, q * d:(q + 1) * d] * decay)
    hf = jnp.concatenate(cols, axis=1)

    @pl.when(i == 0)
    def _():
        nrm_ref[...] = jnp.zeros(nrm_ref.shape, F32)

    nrm_ref[...] += jnp.sum(jnp.abs(hf), axis=0, keepdims=True)
    t0 = (i == 0) & (lax.broadcasted_iota(jnp.int32, hf.shape, 0) == 0)
    col = lax.broadcasted_iota(jnp.int32, hf.shape, 1)
    bwd = ((col >= d) & (col < 2 * d)) | (col >= 3 * d)
    hf_ref[...] = jnp.where(t0 & bwd, 0.0, hf).astype(BF16)


def _hy_filter(z, w1, b1, w2, b2, w3, b3, fr, w4, deltas, tl):
    seq = z.shape[0]
    n4 = w4.shape[1]
    ff = w2.shape[0]
    return pl.pallas_call(
        _hy_filter_kernel,
        grid=(seq // tl,),
        in_specs=[pl.BlockSpec((tl, z.shape[1]), lambda i: (i, 0)),
                  _const_spec(w1.shape), _const_spec((1, ff)), _const_spec(w2.shape), _const_spec((1, ff)),
                  _const_spec(w3.shape), _const_spec((1, ff)), _const_spec((1, ff)), _const_spec(w4.shape),
                  _const_spec(deltas.shape)],
        out_specs=[pl.BlockSpec((tl, n4), lambda i: (i, 0)), _const_spec((1, n4))],
        out_shape=[jax.ShapeDtypeStruct((seq, n4), BF16), jax.ShapeDtypeStruct((1, n4), F32)],
        compiler_params=_params("arbitrary"),
        name="hyena_filter_mlp",
    )(z, w1, b1, w2, b2, w3, b3, fr, w4, deltas)


def _dft_a_kernel(f_ref, u_ref, o_ref):
    o_ref[...] = jnp.dot(f_ref[...], u_ref[...], preferred_element_type=F32).astype(o_ref.dtype)


def _dft_a(f1, u2, tn):
    m, k = f1.shape
    cols = u2.shape[1]
    return pl.pallas_call(
        _dft_a_kernel,
        grid=(cols // tn,),
        in_specs=[_const_spec((m, k)), pl.BlockSpec((k, tn), lambda j: (0, j))],
        out_specs=pl.BlockSpec((m, tn), lambda j: (0, j)),
        out_shape=jax.ShapeDtypeStruct((m, cols), BF16),
        compiler_params=_params("parallel"),
        name="dft_stage_a",
    )(f1, u2)


def _spectrum_kernel(wf_ref, a_ref, inv_ref, k_ref, *, kb, n2):
    inv = inv_ref[...]
    d = inv.shape[1]
    for b in range(kb):
        a = a_ref[:, b].reshape(2 * n2, 2 * d)
        x = jnp.dot(wf_ref[b], a, preferred_element_type=F32)
        xr, xi = x[:n2], x[n2:]
        k_ref[0, b] = (xr[:, :d] + xr[:, d:]) * inv
        k_ref[1, b] = (xi[:, :d] - xi[:, d:]) * inv


def _filter_spectrum(wf, a4, inv_norm, kb):
    _, kh, n2, c4 = a4.shape
    d = c4 // 4
    return pl.pallas_call(
        functools.partial(_spectrum_kernel, kb=kb, n2=n2),
        grid=(2, kh // kb),
        in_specs=[pl.BlockSpec((kb, 2 * n2, 2 * n2), lambda o, j: (j, 0, 0)),
                  pl.BlockSpec((2, kb, n2, 2 * d), lambda o, j: (0, j, 0, o)),
                  pl.BlockSpec((None, 1, d), lambda o, j: (o, 0, 0))],
        out_specs=pl.BlockSpec((None, 2, kb, n2, d), lambda o, j: (o, 0, j, 0, 0)),
        out_shape=jax.ShapeDtypeStruct((2, 2, kh, n2, d), F32),
        compiler_params=_params("parallel", "parallel"),
        name="filter_spectrum",
    )(wf, a4, inv_norm)


def _dft_mid_kernel(wf_ref, wi_ref, a_ref, k_ref, o_ref, *, kb, n2):
    c = a_ref.shape[-1]
    for b in range(kb):
        a = a_ref[:, b].reshape(2 * n2, c)
        x = jnp.dot(wf_ref[b], a, preferred_element_type=F32)
        xr, xi = x[:n2], x[n2:]
        kr, ki = k_ref[0, b], k_ref[1, b]
        y = jnp.concatenate([xr * kr - xi * ki, xr * ki + xi * kr], axis=0).astype(BF16)
        bb = jnp.dot(wi_ref[b], y, preferred_element_type=F32)
        o_ref[:, b] = bb.reshape(2, n2, c).astype(o_ref.dtype)


def _dft_mid(wf, wi, a4, kspec, order, kb):
    _, kh, n2, c = a4.shape
    return pl.pallas_call(
        functools.partial(_dft_mid_kernel, kb=kb, n2=n2),
        grid=(kh // kb,),
        in_specs=[pl.BlockSpec((kb, 2 * n2, 2 * n2), lambda j: (j, 0, 0)),
                  pl.BlockSpec((kb, 2 * n2, 2 * n2), lambda j: (j, 0, 0)),
                  pl.BlockSpec((2, kb, n2, c), lambda j: (0, j, 0, 0)),
                  pl.BlockSpec((None, 2, kb, n2, c), lambda j: (order, 0, j, 0, 0))],
        out_specs=pl.BlockSpec((2, kb, n2, c), lambda j: (0, j, 0, 0)),
        out_shape=jax.ShapeDtypeStruct(a4.shape, BF16),
        compiler_params=_params("parallel"),
        name="dft_middle",
    )(wf, wi, a4, kspec)


def _dft_c_kernel(f_ref, b_ref, u_ref, g_ref, bias_ref, o_ref):
    y = jnp.dot(f_ref[...], b_ref[...], preferred_element_type=F32)
    u = u_ref[...].astype(F32)
    o_ref[...] = (g_ref[...].astype(F32) * (y + u * bias_ref[...])).astype(o_ref.dtype)


def _dft_c(f3, b2, u2, g2, bias_t, tn):
    m, k = f3.shape
    cols = b2.shape[1]
    col = lambda j: (0, j)
    return pl.pallas_call(
        _dft_c_kernel,
        grid=(cols // tn,),
        in_specs=[_const_spec((m, k)), pl.BlockSpec((k, tn), col), pl.BlockSpec((m, tn), col),
                  pl.BlockSpec((m, tn), col), _const_spec((1, tn))],
        out_specs=pl.BlockSpec((m, tn), col),
        out_shape=jax.ShapeDtypeStruct((m, cols), BF16),
        compiler_params=_params("parallel"),
        name="dft_stage_c",
    )(f3, b2, u2, g2, bias_t)


def _dft_constants(seq):
    n = 2 * seq
    n2 = int(round(math.sqrt(seq)))
    n1 = n // n2
    assert n1 * n2 == n and n1 == 2 * n2
    kreal = n1 // 2 + 1
    kh = ((kreal + 7) // 8) * 8
    k1 = jnp.arange(kh, dtype=jnp.int32)
    live = (k1 < kreal)
    nn1 = jnp.arange(n1 // 2, dtype=jnp.int32)
    ph = ((k1[:, None] * nn1[None, :]) % n1).astype(F32) * (2.0 * math.pi / n1)
    lv = live[:, None].astype(F32)
    f1 = jnp.concatenate([jnp.cos(ph) * lv, -jnp.sin(ph) * lv], axis=0).astype(BF16)
    kk2 = jnp.arange(n2, dtype=jnp.int32)
    nn2 = jnp.arange(n2, dtype=jnp.int32)
    num = (k1[:, None, None] * nn2[None, None, :] + n1 * kk2[None, :, None] * nn2[None, None, :]) % n
    ang = num.astype(F32) * (2.0 * math.pi / n)
    mr, mi = jnp.cos(ang), -jnp.sin(ang)
    wf = jnp.concatenate([jnp.concatenate([mr, -mi], axis=2),
                          jnp.concatenate([mi, mr], axis=2)], axis=1).astype(BF16)
    gr, gi = jnp.swapaxes(mr, 1, 2), -jnp.swapaxes(mi, 1, 2)
    wi = jnp.concatenate([jnp.concatenate([gr, -gi], axis=2),
                          jnp.concatenate([gi, gr], axis=2)], axis=1).astype(BF16)
    ck = jnp.where((k1 == 0) | (k1 == n1 // 2), 1.0, 2.0) * live.astype(F32) / n
    ph3 = ((nn1[:, None] * k1[None, :]) % n1).astype(F32) * (2.0 * math.pi / n1)
    f3 = jnp.concatenate([jnp.cos(ph3) * ck[None, :], -jnp.sin(ph3) * ck[None, :]], axis=1).astype(BF16)
    return n1, n2, kh, f1, wf, wi, f3


def _hyena_filter_inputs(seq):
    t = jnp.linspace(0.0, 1.0, seq, dtype=F32)[:, None]
    w = 2.0 * math.pi * jnp.arange(seq, dtype=F32)[:, None] / seq
    f = jnp.linspace(1e-4, HY_BANDS - 1, HY_BANDS, dtype=F32)[None, :]
    z = jnp.concatenate([t, jnp.cos(f * w), -jnp.sin(f * w)], axis=-1)
    return jnp.pad(z, ((0, 0), (0, HY_FILTER_FF - z.shape[1])))


def _hyena_mixer(x, g, sh, sc, gate, w_in, b_in, short_w, short_b, fw1, fb1, fw2, fb2, fw3, fb3,
                 sin_freq, fw4, filter_bias, w_out, b_out):
    seq, d = x.shape
    n1, n2, kh, f1, wf, wi, f3 = _dft_constants(seq)
    kb = 8
    v, g1, g2 = _hy_in(x, g, sh, sc, w_in.astype(BF16), b_in.reshape(1, -1), short_w, short_b.reshape(1, -1),
                       tm=256)

    z = _hyena_filter_inputs(seq)
    fw1p = jnp.pad(fw1, ((0, HY_FILTER_FF - fw1.shape[0]), (0, 0)))
    deltas = jnp.linspace(math.log(HY_DECAY_TARGET) / HY_SLOW_DECAY,
                          math.log(HY_DECAY_TARGET) / HY_FAST_DECAY, d, dtype=F32)[None, :]
    hf, nrm = _hy_filter(z, fw1p, fb1.reshape(1, -1), fw2, fb2.reshape(1, -1), fw3, fb3.reshape(1, -1),
                         sin_freq.reshape(1, -1), fw4.astype(BF16), deltas, tl=512)
    nrm = nrm.reshape(2, 2, d)
    inv_norm = (1.0 / (nrm[:, 0] + nrm[:, 1])).reshape(2, 1, d)
    half = n1 // 2
    fa = _dft_a(f1, hf.reshape(half, n2 * 4 * d), tn=4096)
    kspec = _filter_spectrum(wf, fa.reshape(2, kh, n2, 4 * d), inv_norm, kb)

    def conv(u, gt, order):
        a = _dft_a(f1, u.reshape(half, n2 * d), tn=4096)
        b = _dft_mid(wf, wi, a.reshape(2, kh, n2, d), kspec, order, kb)
        tn = 2 * d
        bias_t = jnp.tile(filter_bias[order].reshape(1, d), (1, tn // d))
        zz = _dft_c(f3, b.reshape(2 * kh, n2 * d), u.reshape(half, n2 * d), gt.reshape(half, n2 * d), bias_t, tn)
        return zz.reshape(seq, d)

    z1 = conv(v, g1, 0)
    z2 = conv(z1, g2, 1)
    return _resid_proj1(z2, w_out.astype(BF16), b_out.reshape(1, -1), x, gate, tm=512)


def _attention_mixer(x, ctx, g, sh, sc, sh_c, sc_c, gate, w_in, q_norm_g, kv_norm_g, w_uq, w_ukv, sink, w_out):
    s, d = x.shape
    n_ctx = ctx.shape[0]
    w_aug, wq, wkv = _attn_weights(w_in, w_uq, w_ukv)
    cos_m, sin_m = _rope_tables(s, n_ctx, MLA_ROPE)
    tot = s + n_ctx
    cm = jnp.concatenate([jnp.ones((tot, MLA_NOPE), F32), cos_m, jnp.zeros((tot, LANES - 96), F32)], axis=1)
    sm = jnp.concatenate([jnp.zeros((tot, MLA_NOPE), F32), sin_m, jnp.zeros((tot, LANES - 96), F32)], axis=1)
    cos_g, sin_g = _rope_tables(s, n_ctx, GQA_HEAD_DIM)
    cg = jnp.concatenate([cos_g, cos_g], axis=1)
    sg = jnp.concatenate([sin_g, sin_g], axis=1)
    sh2 = jnp.stack([sh_c, sh], axis=0)
    sc2 = jnp.stack([sc_c, sc], axis=0)
    mq, mk, mv, gq, gk, gv = _attn_prep(x, ctx, g, sh2, sc2, w_aug, q_norm_g.reshape(1, -1),
                                        kv_norm_g.reshape(1, -1), wq, wkv, cm, sm, cg, sg)
    tk = next(t for t in (3328, 1280, n_ctx) if tot % t == 0)
    mla_out = _mla_attention(mq, mk, mv, tq=min(1024, s), tk=tk)
    gqa_out = _gqa_attention(sink, gq, gk, gv, n_ctx)
    wo = w_out.astype(BF16)
    nm = MLA_HEADS * MLA_V
    return _resid_proj2(mla_out, gqa_out, wo[:nm], wo[nm:], x, gate, tm=512)


def kernel(x, c, ctx, c_ctx, mod_w, mod_b, norm1_g, norm2_g, attn_w_in, mla_q_norm_g, mla_kv_norm_g,
           mla_w_uq, mla_w_ukv, gqa_sink, attn_w_out, hy_w_in, hy_b_in, hy_short_w, hy_short_b,
           hy_ffn_w1, hy_ffn_b1, hy_ffn_w2, hy_ffn_b2, hy_ffn_w3, hy_ffn_b3, hy_sin_freq, hy_ffn_w4,
           hy_filter_bias, hy_w_out, hy_b_out, moe_router_w, moe_router_b, moe_w1, moe_b1, moe_w2,
           moe_b2, final_norm_g):
    batch, seq, d = x.shape
    assert batch == 1
    depth = mod_w.shape[0]
    xs = x.reshape(seq, d)
    cs = ctx.reshape(-1, d)
    mod = _modulation(c, c_ctx, mod_w, mod_b)
    fg = final_norm_g.reshape(1, d)
    for i in range(depth):
        m = mod[i, 0].reshape(6, 1, d)
        mc = mod[i, 1].reshape(6, 1, d)
        g1 = norm1_g[i].reshape(1, d)
        j = i // 2
        if i % 2 == 0:
            xs = _attention_mixer(xs, cs, g1, m[0], m[1], mc[0], mc[1], m[2], attn_w_in[j], mla_q_norm_g[j],
                                  mla_kv_norm_g[j], mla_w_uq[j], mla_w_ukv[j], gqa_sink[j], attn_w_out[j])
        else:
            xs = _hyena_mixer(xs, g1, m[0], m[1], m[2], hy_w_in[j], hy_b_in[j], hy_short_w[j], hy_short_b[j],
                              hy_ffn_w1[j], hy_ffn_b1[j], hy_ffn_w2[j], hy_ffn_b2[j], hy_ffn_w3[j], hy_ffn_b3[j],
                              hy_sin_freq[j], hy_ffn_w4[j], hy_filter_bias[j], hy_w_out[j], hy_b_out[j])
        xs = _moe_layer(xs, norm2_g[i].reshape(1, d), m[3], m[4], m[5], moe_router_w[i], moe_router_b[i],
                        moe_w1, moe_b1, moe_w2, moe_b2, fg, final_norm=(i == depth - 1), layer=i)
    return xs.reshape(batch, seq, d)
```

```python
import functools
import math

import jax
import jax.numpy as jnp
from jax import lax
from jax.experimental import pallas as pl
from jax.experimental.pallas import tpu as pltpu

F32 = jnp.float32
BF16 = jnp.bfloat16

RMS_EPS = 1e-6
ROPE_BASE = 10000.0
GRID_W = 64
NEG_INF = -1e30

MLA_HEADS = 8
MLA_Q_RANK = 256
MLA_KV_RANK = 128
MLA_NOPE = 64
MLA_ROPE = 32
MLA_V = 64
MLA_SCALE = 1.0 / math.sqrt(MLA_NOPE + MLA_ROPE)
LOG2_E = math.log2(math.e)

GQA_Q_HEADS = 8
GQA_KV_HEADS = 2
GQA_HEAD_DIM = 64
GQA_SCALE = 1.0 / math.sqrt(GQA_HEAD_DIM)
WIN_BLOCK = 128

OFF_CKV = 0
OFF_KROPE = OFF_CKV + MLA_KV_RANK
OFF_GK = OFF_KROPE + MLA_ROPE
OFF_GV = OFF_GK + GQA_KV_HEADS * GQA_HEAD_DIM
KV_COLS = OFF_GV + GQA_KV_HEADS * GQA_HEAD_DIM
OFF_CQ = KV_COLS
OFF_GQ = OFF_CQ + MLA_Q_RANK
PROJ_COLS = OFF_GQ + GQA_Q_HEADS * GQA_HEAD_DIM

HY_BANDS = 16
HY_FILTER_FF = 64
HY_DECAY_TARGET = 1e-2
HY_FAST_DECAY = 0.3
HY_SLOW_DECAY = 1.5

N_EXPERTS = 32
TOP_K = 4
SWIGLU_LIMIT = 7.0
SWIGLU_ALPHA = 1.702

LANES = 128
VMEM_LIMIT_BYTES = 56 * 1024 * 1024

C_CKV = 0
C_CQ = C_CKV + 128
C_GQ = C_CQ + 256
C_GQR = C_GQ + 512
C_GK = C_GQR + 512
C_GKR = C_GK + 512
C_GV = C_GKR + 512
C_KR = C_GV + 256
C_KRR = C_KR + 128
AUG_COLS = C_KRR + 128


def _params(*sem):
    return pltpu.CompilerParams(dimension_semantics=sem, vmem_limit_bytes=VMEM_LIMIT_BYTES)


def _const_spec(shape):
    n = len(shape)
    return pl.BlockSpec(shape, lambda *_: (0,) * n)


def _norm_mod(x, g, sh, sc):
    ms = jnp.mean(x * x, axis=-1, keepdims=True)
    return (x * lax.rsqrt(ms + RMS_EPS) * g) * (1.0 + sc) + sh


def _rms(x, g):
    ms = jnp.mean(x * x, axis=-1, keepdims=True)
    return x * lax.rsqrt(ms + RMS_EPS) * g


def _mod_kernel(cc_ref, w_ref, b_ref, o_ref):
    cc = cc_ref[...]
    s = cc / (1.0 + jnp.exp(-cc))
    w = w_ref[...]
    b = b_ref[...]
    o_ref[0:1, :] = jnp.sum(w * s[:, 0:1], axis=0, keepdims=True) + b
    o_ref[1:2, :] = jnp.sum(w * s[:, 1:2], axis=0, keepdims=True) + b


def _modulation(c, c_ctx, mod_w, mod_b):
    depth, d, n = mod_w.shape
    tn = 1536
    cc = jnp.stack([c.reshape(d), c_ctx.reshape(d)], axis=1)
    return pl.pallas_call(
        _mod_kernel,
        grid=(depth, n // tn),
        in_specs=[
            _const_spec((d, 2)),
            pl.BlockSpec((None, d, tn), lambda l, j: (l, 0, j)),
            pl.BlockSpec((None, 1, tn), lambda l, j: (l, 0, j)),
        ],
        out_specs=pl.BlockSpec((None, 2, tn), lambda l, j: (l, 0, j)),
        out_shape=jax.ShapeDtypeStruct((depth, 2, n), F32),
        compiler_params=_params("parallel", "parallel"),
        name="modulation",
    )(cc, mod_w, mod_b.reshape(depth, 1, n))


def _rot_cols_1d(w):
    half = w.shape[-1] // 2
    return jnp.concatenate([-w[..., half:], w[..., :half]], axis=-1)


def _rot_cols_2d(w):
    half = w.shape[-1] // 2
    return jnp.concatenate([_rot_cols_1d(w[..., :half]), _rot_cols_1d(w[..., half:])], axis=-1)


def _rope_tables(n_lat, n_ctx, dims):
    t = jnp.arange(n_lat, dtype=jnp.int32)
    row = (t // GRID_W).astype(F32)
    col = (t % GRID_W).astype(F32)
    q = dims // 4
    inv = ROPE_BASE ** (-jnp.arange(q, dtype=F32) / q)
    ar = row[:, None] * inv[None, :]
    ac = col[:, None] * inv[None, :]
    cos = jnp.concatenate([jnp.cos(ar), jnp.cos(ar), jnp.cos(ac), jnp.cos(ac)], axis=-1)
    sin = jnp.concatenate([jnp.sin(ar), jnp.sin(ar), jnp.sin(ac), jnp.sin(ac)], axis=-1)
    cos = jnp.concatenate([jnp.ones((n_ctx, dims), F32), cos], axis=0)
    sin = jnp.concatenate([jnp.zeros((n_ctx, dims), F32), sin], axis=0)
    return cos, sin


def _attn_weights(w_in, w_uq, w_ukv):
    d = w_in.shape[0]
    z64 = jnp.zeros((d, 64), F32)
    z32 = jnp.zeros((d, 32), F32)
    ckv = w_in[:, OFF_CKV:OFF_KROPE]
    krope = w_in[:, OFF_KROPE:OFF_GK]
    gk = w_in[:, OFF_GK:OFF_GV]
    gv = w_in[:, OFF_GV:KV_COLS]
    cq = w_in[:, OFF_CQ:OFF_GQ]
    gq = w_in[:, OFF_GQ:PROJ_COLS]
    gq_rot = _rot_cols_2d(gq.reshape(d, GQA_Q_HEADS, GQA_HEAD_DIM)).reshape(d, -1)
    k0, k1 = gk[:, :64], gk[:, 64:]
    k0r, k1r = _rot_cols_2d(k0), _rot_cols_2d(k1)
    gk4 = jnp.concatenate([k0, z64, z64, k0, k1, z64, z64, k1], axis=1)
    gk4r = jnp.concatenate([k0r, z64, z64, k0r, k1r, z64, z64, k1r], axis=1)
    v0, v1 = gv[:, :64], gv[:, 64:]
    gv2 = jnp.concatenate([v0, v0, v1, v1], axis=1)
    kr = jnp.concatenate([z64, krope, z32], axis=1)
    krr = jnp.concatenate([z64, _rot_cols_2d(krope), z32], axis=1)
    w_aug = jnp.concatenate([ckv, cq, gq, gq_rot, gk4, gk4r, gv2, kr, krr], axis=1).astype(BF16)

    r = w_uq.shape[0]
    q3 = w_uq.reshape(r, MLA_HEADS, MLA_NOPE + MLA_ROPE)
    q_main = jnp.pad(q3, ((0, 0), (0, 0), (0, LANES - MLA_NOPE - MLA_ROPE)))
    q_rot = jnp.concatenate([jnp.zeros((r, MLA_HEADS, MLA_NOPE), F32), _rot_cols_2d(q3[..., MLA_NOPE:]),
                             jnp.zeros((r, MLA_HEADS, LANES - MLA_NOPE - MLA_ROPE), F32)], axis=-1)
    wq = jnp.concatenate([q_main.reshape(r, -1), q_rot.reshape(r, -1)], axis=1).astype(BF16)

    rk = w_ukv.shape[0]
    kv3 = w_ukv.reshape(rk, MLA_HEADS, MLA_NOPE + MLA_V)
    k_part = jnp.pad(kv3[..., :MLA_NOPE], ((0, 0), (0, 0), (0, LANES - MLA_NOPE))).reshape(rk, -1)
    v_part = kv3[..., MLA_NOPE:].reshape(rk, -1)
    wkv = jnp.concatenate([k_part, v_part], axis=1).astype(BF16)
    return w_aug, wq, wkv


def _attn_prep_kernel(x_ref, ctx_ref, g_ref, sh_ref, sc_ref, w_ref, gq_ref, gkv_ref, wq_ref, wkv_ref,
                      cm_ref, sm_ref, cg_ref, sg_ref,
                      mq_ref, mk_ref, mv_ref, gqo_ref, gko_ref, gvo_ref):
    i = pl.program_id(0)
    xin = jnp.where(i == 0, ctx_ref[...], x_ref[...])
    h = _norm_mod(xin, g_ref[...], sh_ref[...], sc_ref[...])
    p = jnp.dot(h.astype(BF16), w_ref[...], preferred_element_type=F32)

    ckv = _rms(p[:, C_CKV:C_CKV + 128], gkv_ref[...])
    kv = jnp.dot(ckv.astype(BF16), wkv_ref[...], preferred_element_type=F32)
    cq = _rms(p[:, C_CQ:C_CQ + 256], gq_ref[...])
    qa = jnp.dot(cq.astype(BF16), wq_ref[...], preferred_element_type=F32)

    cm, sm = cm_ref[...], sm_ref[...]
    kr = p[:, C_KR:C_KR + 128] * cm + p[:, C_KRR:C_KRR + 128] * sm
    for hd in range(MLA_HEADS):
        lo = hd * LANES
        qh = qa[:, lo:lo + LANES] * cm + qa[:, 1024 + lo:1024 + lo + LANES] * sm
        mq_ref[hd] = (qh * (MLA_SCALE * LOG2_E)).astype(BF16)
        mk_ref[hd] = (kv[:, lo:lo + LANES] + kr).astype(BF16)
    mv_ref[...] = kv[:, 1024:1536].astype(BF16)

    cg, sg = cg_ref[...], sg_ref[...]
    for gI in range(4):
        lo = gI * LANES
        gq = p[:, C_GQ + lo:C_GQ + lo + LANES] * cg + p[:, C_GQR + lo:C_GQR + lo + LANES] * sg
        gqo_ref[:, lo:lo + LANES] = (gq * GQA_SCALE).astype(BF16)
        gk = p[:, C_GK + lo:C_GK + lo + LANES] * cg + p[:, C_GKR + lo:C_GKR + lo + LANES] * sg
        gko_ref[:, lo:lo + LANES] = gk.astype(BF16)
    gvo_ref[...] = p[:, C_GV:C_GV + 256].astype(BF16)


def _attn_prep(x, ctx, g, sh2, sc2, w_aug, gq, gkv, wq, wkv, cm, sm, cg, sg):
    s, d = x.shape
    c = ctx.shape[0]
    tm = c
    nt = s // tm + 1
    tot = s + c
    lat = lambda i: (jnp.maximum(i - 1, 0), 0)
    row = lambda i: (i, 0)
    return pl.pallas_call(
        _attn_prep_kernel,
        grid=(nt,),
        in_specs=[
            pl.BlockSpec((tm, d), lat),
            _const_spec((c, d)),
            _const_spec((1, d)),
            pl.BlockSpec((None, 1, d), lambda i: (jnp.minimum(i, 1), 0, 0)),
            pl.BlockSpec((None, 1, d), lambda i: (jnp.minimum(i, 1), 0, 0)),
            _const_spec(w_aug.shape),
            _const_spec(gq.shape),
            _const_spec(gkv.shape),
            _const_spec(wq.shape),
            _const_spec(wkv.shape),
            pl.BlockSpec((tm, LANES), row),
            pl.BlockSpec((tm, LANES), row),
            pl.BlockSpec((tm, LANES), row),
            pl.BlockSpec((tm, LANES), row),
        ],
        out_specs=[
            pl.BlockSpec((MLA_HEADS, tm, LANES), lambda i: (0, jnp.maximum(i - 1, 0), 0)),
            pl.BlockSpec((MLA_HEADS, tm, LANES), lambda i: (0, i, 0)),
            pl.BlockSpec((tm, 512), row),
            pl.BlockSpec((tm, 512), lat),
            pl.BlockSpec((tm, 512), row),
            pl.BlockSpec((tm, 256), row),
        ],
        out_shape=[
            jax.ShapeDtypeStruct((MLA_HEADS, s, LANES), BF16),
            jax.ShapeDtypeStruct((MLA_HEADS, tot, LANES), BF16),
            jax.ShapeDtypeStruct((tot, 512), BF16),
            jax.ShapeDtypeStruct((s, 512), BF16),
            jax.ShapeDtypeStruct((tot, 512), BF16),
            jax.ShapeDtypeStruct((tot, 256), BF16),
        ],
        compiler_params=_params("arbitrary"),
        name="attn_prep",
    )(x, ctx, g, sh2, sc2, w_aug, gq, gkv, wq, wkv, cm, sm, cg, sg)


MLA_Q_SUB = 128


def _mla_kernel(q_ref, k_ref, v_ref, o_ref, m_sc, acc_sc):
    j = pl.program_id(2)

    @pl.when(j == 0)
    def _():
        m_sc[...] = jnp.full(m_sc.shape, -jnp.inf, F32)
        acc_sc[...] = jnp.zeros(acc_sc.shape, F32)

    v = v_ref[...]
    v1 = jnp.concatenate([v, jnp.ones(v.shape, BF16)], axis=1)
    tq = q_ref.shape[1]
    sub = min(MLA_Q_SUB, tq)
    for hh in range(2):
        k = k_ref[hh]
        for r0 in range(0, tq, sub):
            rows = pl.ds(r0, sub)
            s = lax.dot_general(q_ref[hh, rows, :], k, (((1,), (1,)), ((), ())),
                                preferred_element_type=F32)
            m_prev = m_sc[hh, rows, :]
            m_new = jnp.maximum(m_prev, jnp.max(s, axis=-1, keepdims=True))
            alpha = jnp.exp2(m_prev - m_new)
            p = jnp.exp2(s - m_new).astype(BF16)
            acc_sc[hh, rows, :] = alpha * acc_sc[hh, rows, :] + jnp.dot(p, v1, preferred_element_type=F32)
            m_sc[hh, rows, :] = m_new

    @pl.when(j == pl.num_programs(2) - 1)
    def _():
        lane = lax.broadcasted_iota(jnp.int32, (acc_sc.shape[1], LANES), 1)
        o0 = acc_sc[0, :, :LANES] / acc_sc[0, :, LANES:]
        o1 = acc_sc[1, :, :LANES] / acc_sc[1, :, LANES:]
        o_ref[...] = jnp.where(lane < MLA_V, o0, o1).astype(o_ref.dtype)


def _mla_attention(mq, mk, mv, tq, tk):
    _, s, _ = mq.shape
    tot = mk.shape[1]
    return pl.pallas_call(
        _mla_kernel,
        grid=(MLA_HEADS // 2, s // tq, tot // tk),
        in_specs=[
            pl.BlockSpec((2, tq, LANES), lambda h, i, j: (h, i, 0)),
            pl.BlockSpec((2, tk, LANES), lambda h, i, j: (h, j, 0)),
            pl.BlockSpec((tk, LANES), lambda h, i, j: (j, h)),
        ],
        out_specs=pl.BlockSpec((tq, LANES), lambda h, i, j: (i, h)),
        out_shape=jax.ShapeDtypeStruct((s, MLA_HEADS * MLA_V), BF16),
        scratch_shapes=[
            pltpu.VMEM((2, tq, 1), F32),
            pltpu.VMEM((2, tq, 2 * LANES), F32),
        ],
        compiler_params=_params("parallel", "parallel", "arbitrary"),
        name="mla_flash",
    )(mq, mk, mv)


def _gqa_kernel(sink_ref, q_ref, kp_ref, ko_ref, kn_ref, kc_ref, vp_ref, vo_ref, vn_ref, vc_ref, o_ref):
    n = pl.program_id(0)
    nb = pl.num_programs(0)
    wb = WIN_BLOCK
    row2 = lax.broadcasted_iota(jnp.int32, (2 * wb, wb), 0)
    r = row2 & (wb - 1)
    c = lax.broadcasted_iota(jnp.int32, (2 * wb, wb), 1)
    ok_prev = c >= r + jnp.where(n >= 1, 0, wb)
    ok_next = c + jnp.where(n <= nb - 2, 0, wb) <= r
    lane = lax.broadcasted_iota(jnp.int32, (2 * wb, LANES), 1)
    first = lax.broadcasted_iota(jnp.int32, (2 * wb, 1), 0) < wb
    dn = (((1,), (1,)), ((), ()))
    for kvh in range(GQA_KV_HEADS):
        q = jnp.concatenate([q_ref[:, (2 * kvh) * LANES:(2 * kvh + 1) * LANES],
                             q_ref[:, (2 * kvh + 1) * LANES:(2 * kvh + 2) * LANES]], axis=0)
        vlo = kvh * LANES
        outs = []
        for var in range(2):
            klo = (2 * kvh + var) * LANES
            s_p = lax.dot_general(q, kp_ref[:, klo:klo + LANES], dn, preferred_element_type=F32)
            s_o = lax.dot_general(q, ko_ref[:, klo:klo + LANES], dn, preferred_element_type=F32)
            s_n = lax.dot_general(q, kn_ref[:, klo:klo + LANES], dn, preferred_element_type=F32)
            s_c = lax.dot_general(q, kc_ref[:, klo:klo + LANES], dn, preferred_element_type=F32)
            s_p = jnp.where(ok_prev, s_p, NEG_INF)
            s_n = jnp.where(ok_next, s_n, NEG_INF)
            sink = jnp.where(first, sink_ref[4 * kvh + var], sink_ref[4 * kvh + 2 + var])
            m = jnp.maximum(jnp.maximum(jnp.max(s_p, axis=-1, keepdims=True),
                                        jnp.max(s_o, axis=-1, keepdims=True)),
                            jnp.maximum(jnp.max(s_n, axis=-1, keepdims=True),
                                        jnp.max(s_c, axis=-1, keepdims=True)))
            m = jnp.maximum(m, sink)
            e_p, e_o, e_n, e_c = jnp.exp(s_p - m), jnp.exp(s_o - m), jnp.exp(s_n - m), jnp.exp(s_c - m)
            den = (jnp.sum(e_p, axis=-1, keepdims=True) + jnp.sum(e_o, axis=-1, keepdims=True)
                   + jnp.sum(e_n, axis=-1, keepdims=True) + jnp.sum(e_c, axis=-1, keepdims=True)
                   + jnp.exp(sink - m))
            o = (jnp.dot(e_p.astype(BF16), vp_ref[:, vlo:vlo + LANES], preferred_element_type=F32)
                 + jnp.dot(e_o.astype(BF16), vo_ref[:, vlo:vlo + LANES], preferred_element_type=F32)
                 + jnp.dot(e_n.astype(BF16), vn_ref[:, vlo:vlo + LANES], preferred_element_type=F32)
                 + jnp.dot(e_c.astype(BF16), vc_ref[:, vlo:vlo + LANES], preferred_element_type=F32))
            outs.append(o / den)
        o2 = jnp.where(lane < GQA_HEAD_DIM, outs[0], outs[1]).astype(o_ref.dtype)
        o_ref[:, (2 * kvh) * LANES:(2 * kvh + 1) * LANES] = o2[:wb]
        o_ref[:, (2 * kvh + 1) * LANES:(2 * kvh + 2) * LANES] = o2[wb:]


def _gqa_attention(sink, gq, gk, gv, n_ctx):
    s = gq.shape[0]
    wb = WIN_BLOCK
    nb = s // wb
    cb = n_ctx // wb
    prev = lambda n: (n + cb - 1, 0)
    own = lambda n: (n + cb, 0)
    nxt = lambda n: (jnp.minimum(n + cb + 1, nb + cb - 1), 0)
    return pl.pallas_call(
        _gqa_kernel,
        grid=(nb,),
        in_specs=[
            pl.BlockSpec(memory_space=pltpu.SMEM),
            pl.BlockSpec((wb, 512), lambda n: (n, 0)),
            pl.BlockSpec((wb, 512), prev),
            pl.BlockSpec((wb, 512), own),
            pl.BlockSpec((wb, 512), nxt),
            _const_spec((n_ctx, 512)),
            pl.BlockSpec((wb, 256), prev),
            pl.BlockSpec((wb, 256), own),
            pl.BlockSpec((wb, 256), nxt),
            _const_spec((n_ctx, 256)),
        ],
        out_specs=pl.BlockSpec((wb, 512), lambda n: (n, 0)),
        out_shape=jax.ShapeDtypeStruct((s, 512), BF16),
        compiler_params=_params("parallel"),
        name="gqa_window",
    )(sink, gq, gk, gk, gk, gk, gv, gv, gv, gv)


def _resid_proj2_kernel(a_ref, b_ref, wa_ref, wb_ref, x_ref, gt_ref, o_ref):
    y = (jnp.dot(a_ref[...], wa_ref[...], preferred_element_type=F32)
         + jnp.dot(b_ref[...], wb_ref[...], preferred_element_type=F32))
    o_ref[...] = x_ref[...] + gt_ref[...] * y


def _resid_proj2(a, b, wa, wb, x, gate, tm):
    s, d = x.shape
    row = lambda i: (i, 0)
    return pl.pallas_call(
        _resid_proj2_kernel,
        grid=(s // tm,),
        in_specs=[pl.BlockSpec((tm, a.shape[1]), row), pl.BlockSpec((tm, b.shape[1]), row),
                  _const_spec(wa.shape), _const_spec(wb.shape),
                  pl.BlockSpec((tm, d), row), _const_spec((1, d))],
        out_specs=pl.BlockSpec((tm, d), row),
        out_shape=jax.ShapeDtypeStruct((s, d), F32),
        compiler_params=_params("parallel"),
        name="attn_out_proj",
    )(a, b, wa, wb, x, gate)


def _resid_proj1_kernel(a_ref, w_ref, bias_ref, x_ref, gt_ref, o_ref):
    y = jnp.dot(a_ref[...], w_ref[...], preferred_element_type=F32) + bias_ref[...]
    o_ref[...] = x_ref[...] + gt_ref[...] * y


def _resid_proj1(a, w, bias, x, gate, tm):
    s, d = x.shape
    row = lambda i: (i, 0)
    return pl.pallas_call(
        _resid_proj1_kernel,
        grid=(s // tm,),
        in_specs=[pl.BlockSpec((tm, a.shape[1]), row), _const_spec(w.shape), _const_spec((1, d)),
                  pl.BlockSpec((tm, d), row), _const_spec((1, d))],
        out_specs=pl.BlockSpec((tm, d), row),
        out_shape=jax.ShapeDtypeStruct((s, d), F32),
        compiler_params=_params("parallel"),
        name="hyena_out_proj",
    )(a, w, bias, x, gate)


META_IDX, META_W, META_RANK = 0, 4, 8


def _router_kernel(x_ref, g_ref, sh_ref, sc_ref, rw_ref, rb_ref, t_ref, meta_ref, cnt_ref, carry_sc):
    i = pl.program_id(0)

    @pl.when(i == 0)
    def _():
        carry_sc[...] = jnp.zeros(carry_sc.shape, F32)

    t = _norm_mod(x_ref[...], g_ref[...], sh_ref[...], sc_ref[...])
    t_ref[...] = t
    logits = jnp.dot(t, rw_ref[...], preferred_element_type=F32,
                     precision=lax.Precision.HIGHEST) + rb_ref[...]
    tm = logits.shape[0]
    lane = lax.broadcasted_iota(jnp.int32, logits.shape, 1)
    work = logits
    vals, hots, idxs = [], [], []
    for _ in range(TOP_K):
        m = jnp.max(work, axis=-1, keepdims=True)
        idx = jnp.min(jnp.where(work == m, lane, N_EXPERTS), axis=-1, keepdims=True)
        hot = lane == idx
        vals.append(m)
        hots.append(hot)
        idxs.append(idx)
        work = jnp.where(hot, -jnp.inf, work)
    es = [jnp.exp(v - vals[0]) for v in vals]
    den = es[0] + es[1] + es[2] + es[3]
    hot_all = jnp.zeros(logits.shape, F32)
    for hot in hots:
        hot_all = hot_all + jnp.where(hot, 1.0, 0.0)
    r = lax.broadcasted_iota(jnp.int32, (tm, tm), 0)
    c = lax.broadcasted_iota(jnp.int32, (tm, tm), 1)
    tri = jnp.where(r > c, 1.0, 0.0).astype(BF16)
    rank = jnp.dot(tri, hot_all.astype(BF16), preferred_element_type=F32) + carry_sc[...]
    carry_sc[...] += jnp.sum(hot_all, axis=0, keepdims=True)
    cnt_ref[...] = carry_sc[...]
    lane_m = lax.broadcasted_iota(jnp.int32, (tm, LANES), 1)
    meta = jnp.zeros((tm, LANES), F32)
    for k in range(TOP_K):
        rk = jnp.sum(jnp.where(hots[k], rank, 0.0), axis=-1, keepdims=True)
        meta = jnp.where(lane_m == META_IDX + k, idxs[k].astype(F32), meta)
        meta = jnp.where(lane_m == META_W + k, es[k] / den, meta)
        meta = jnp.where(lane_m == META_RANK + k, rk, meta)
    meta_ref[...] = meta


def _router(x, g, sh, sc, rw, rb, tm):
    s, d = x.shape
    e = rw.shape[1]
    row = lambda i: (i, 0)
    return pl.pallas_call(
        _router_kernel,
        grid=(s // tm,),
        in_specs=[pl.BlockSpec((tm, d), row), _const_spec((1, d)), _const_spec((1, d)), _const_spec((1, d)),
                  _const_spec((d, e)), _const_spec((1, e))],
        out_specs=[pl.BlockSpec((tm, d), row), pl.BlockSpec((tm, LANES), row), _const_spec((1, e))],
        out_shape=[jax.ShapeDtypeStruct((s, d), F32), jax.ShapeDtypeStruct((s, LANES), F32),
                   jax.ShapeDtypeStruct((1, e), F32)],
        scratch_shapes=[pltpu.VMEM((1, e), F32)],
        compiler_params=_params("arbitrary"),
        name="moe_router",
    )(x, g, sh, sc, rw, rb)


def _rows_copy(src, dst, sem, n):
    return pltpu.make_async_copy(src.at[pl.ds(0, n)], dst.at[pl.ds(0, n)], sem)


def _dispatch_kernel(pos_ref, t_ref, xs_init_ref, xs_ref, sem, *, tm):
    del xs_init_ref
    base = pl.program_id(0) * tm

    def body(r, carry):
        for k in range(TOP_K):
            p = pos_ref[(base + r) * TOP_K + k]
            pltpu.make_async_copy(t_ref.at[pl.ds(r, 1)], xs_ref.at[pl.ds(p, 1)], sem).start(priority=k % 2)
        return carry

    lax.fori_loop(0, tm, body, 0, unroll=8)
    for k in range(TOP_K):
        _rows_copy(t_ref, xs_ref, sem, tm).wait()


def _dispatch(pos, t, rows, tm):
    s, d = t.shape
    grid_spec = pltpu.PrefetchScalarGridSpec(
        num_scalar_prefetch=1,
        grid=(s // tm,),
        in_specs=[pl.BlockSpec((tm, d), lambda i, pos: (i, 0)),
                  pl.BlockSpec(memory_space=pl.ANY)],
        out_specs=pl.BlockSpec(memory_space=pl.ANY),
        scratch_shapes=[pltpu.SemaphoreType.DMA(())],
    )
    return pl.pallas_call(
        functools.partial(_dispatch_kernel, tm=tm),
        grid_spec=grid_spec,
        out_shape=jax.ShapeDtypeStruct((rows, d), F32),
        input_output_aliases={2: 0},
        compiler_params=_params("arbitrary"),
        name="moe_dispatch",
    )(pos, t, jnp.zeros((rows, d), F32))


MOE_ROW_SUB = 256


def _moe_expert_kernel(te_ref, nu_ref, xs_ref, w1_ref, b1_ref, w2_ref, b2_ref, y_ref, w1b, w2b, *, ff):
    i = pl.program_id(0)
    e = te_ref[i]
    e_prev = te_ref[jnp.maximum(i - 1, 0)]

    @pl.when((i == 0) | (e != e_prev))
    def _():
        w1b[...] = w1_ref[...].astype(BF16)
        w2b[...] = w2_ref[...].astype(BF16)

    @pl.when(i < nu_ref[0])
    def _():
        for r0 in range(0, xs_ref.shape[0], MOE_ROW_SUB):
            rows = pl.ds(r0, MOE_ROW_SUB)
            u = jnp.dot(xs_ref[rows, :].astype(BF16), w1b[...], preferred_element_type=F32) + b1_ref[...]
            glu = jnp.minimum(u[:, :ff], SWIGLU_LIMIT)
            lin = jnp.clip(u[:, ff:], -SWIGLU_LIMIT, SWIGLU_LIMIT)
            act = glu / (1.0 + jnp.exp(-SWIGLU_ALPHA * glu)) * (lin + 1.0)
            y_ref[rows, :] = jnp.dot(act.astype(BF16), w2b[...], preferred_element_type=F32) + b2_ref[...]

    @pl.when(i >= nu_ref[0])
    def _():
        y_ref[...] = jnp.zeros(y_ref.shape, F32)


def _moe_experts(tile_expert, n_used, xs, w1, b1, w2, b2, tmm, layer):
    rows, d = xs.shape
    _, ne, _, ff2 = w1.shape
    ff = ff2 // 2
    wsel = lambda i, te, nu: (layer, te[i], 0, 0)
    grid_spec = pltpu.PrefetchScalarGridSpec(
        num_scalar_prefetch=2,
        grid=(rows // tmm,),
        in_specs=[pl.BlockSpec((tmm, d), lambda i, te, nu: (jnp.minimum(i, nu[0] - 1), 0)),
                  pl.BlockSpec((None, None, d, ff2), wsel),
                  pl.BlockSpec((None, None, 1, ff2), wsel),
                  pl.BlockSpec((None, None, ff, d), wsel),
                  pl.BlockSpec((None, None, 1, d), wsel)],
        out_specs=pl.BlockSpec((tmm, d), lambda i, te, nu: (i, 0)),
        scratch_shapes=[pltpu.VMEM((d, ff2), BF16), pltpu.VMEM((ff, d), BF16)],
    )
    return pl.pallas_call(
        functools.partial(_moe_expert_kernel, ff=ff),
        grid_spec=grid_spec,
        out_shape=jax.ShapeDtypeStruct((rows, d), F32),
        compiler_params=_params("arbitrary"),
        name="moe_experts",
    )(tile_expert, n_used, xs, w1, b1.reshape(-1, ne, 1, ff2), w2, b2.reshape(-1, ne, 1, d))


def _combine_kernel(pos_ref, y_ref, meta_ref, x_ref, gt_ref, fg_ref, o_ref, buf, sem, *, tc, final_norm):
    base = pl.program_id(0) * tc

    def body(r, carry):
        for k in range(TOP_K):
            p = pos_ref[(base + r) * TOP_K + k]
            pltpu.make_async_copy(y_ref.at[pl.ds(p, 1)], buf.at[k, pl.ds(r, 1)], sem).start(priority=k % 2)
        return carry

    lax.fori_loop(0, tc, body, 0, unroll=8)
    for k in range(TOP_K):
        _rows_copy(y_ref, buf.at[k], sem, tc).wait()
    meta = meta_ref[...]
    acc = buf[0] * meta[:, META_W:META_W + 1]
    for k in range(1, TOP_K):
        acc = acc + buf[k] * meta[:, META_W + k:META_W + k + 1]
    xo = x_ref[...] + gt_ref[...] * acc
    if final_norm:
        xo = _rms(xo, fg_ref[...])
    o_ref[...] = xo


def _combine(pos, y, meta, x, gate, final_g, tc, final_norm):
    s, d = x.shape
    row = lambda i, pos: (i, 0)
    grid_spec = pltpu.PrefetchScalarGridSpec(
        num_scalar_prefetch=1,
        grid=(s // tc,),
        in_specs=[pl.BlockSpec(memory_space=pl.ANY),
                  pl.BlockSpec((tc, LANES), row),
                  pl.BlockSpec((tc, d), row),
                  pl.BlockSpec((1, d), lambda i, pos: (0, 0)),
                  pl.BlockSpec((1, d), lambda i, pos: (0, 0))],
        out_specs=pl.BlockSpec((tc, d), row),
        scratch_shapes=[pltpu.VMEM((TOP_K, tc, d), F32), pltpu.SemaphoreType.DMA(())],
    )
    return pl.pallas_call(
        functools.partial(_combine_kernel, tc=tc, final_norm=final_norm),
        grid_spec=grid_spec,
        out_shape=jax.ShapeDtypeStruct((s, d), F32),
        compiler_params=_params("arbitrary"),
        name="moe_combine",
    )(pos, y, meta, x, gate, final_g)


MOE_ROW_TILE = 256


def _moe_layer(x, g, sh, sc, gate, rw, rb, w1, b1, w2, b2, final_g, final_norm, layer):
    s, d = x.shape
    ne = rw.shape[1]
    tmm = MOE_ROW_TILE
    t, meta, cnt = _router(x, g, sh, sc, rw, rb.reshape(1, -1), tm=512)
    idx = meta[:, META_IDX:META_IDX + TOP_K].astype(jnp.int32)
    rank = meta[:, META_RANK:META_RANK + TOP_K].astype(jnp.int32)
    counts = cnt[0].astype(jnp.int32)
    padded = ((counts + tmm - 1) // tmm) * tmm
    ends = jnp.cumsum(padded)
    pos = (jnp.take(ends - padded, idx) + rank).reshape(-1)
    rows = TOP_K * s + ne * tmm
    nt = rows // tmm
    tile_ends = ends // tmm
    n_used = tile_ends[-1]
    tiles = jnp.arange(nt, dtype=jnp.int32)
    last = jnp.maximum(n_used - 1, 0)
    te = jnp.sum((jnp.minimum(tiles, last)[:, None] >= tile_ends[None, :]).astype(jnp.int32), axis=1)
    te = jnp.minimum(te, ne - 1)
    xs = _dispatch(pos, t, rows, tm=512)
    y = _moe_experts(te, n_used.reshape(1), xs, w1, b1, w2, b2, tmm, layer)
    return _combine(pos, y, meta, x, gate, final_g, tc=256, final_norm=final_norm)


def _hy_in_kernel(xp_ref, x_ref, xn_ref, g_ref, sh_ref, sc_ref, w_ref, b_ref, sw_ref, sb_ref,
                  v_ref, g1_ref, g2_ref, *, tm, seq):
    i = pl.program_id(0)
    d = x_ref.shape[1]
    xe = jnp.concatenate([xp_ref[...], x_ref[...], xn_ref[...]], axis=0)
    h = _norm_mod(xe, g_ref[...], sh_ref[...], sc_ref[...])
    p = jnp.dot(h.astype(BF16), w_ref[...], preferred_element_type=F32) + b_ref[...]
    row = i * tm - 8 + lax.broadcasted_iota(jnp.int32, (tm + 16, 1), 0)
    p = jnp.where((row >= 0) & (row < seq), p, 0.0)
    pm = pltpu.roll(p, 1, 0)[8:8 + tm]
    pc = p[8:8 + tm]
    pn = pltpu.roll(p, tm + 15, 0)[8:8 + tm]
    sw = sw_ref[...]
    out = sw[0:1] * pm + sw[1:2] * pc + sw[2:3] * pn + sb_ref[...]
    v_ref[...] = out[:, :d].astype(BF16)
    g1_ref[...] = out[:, d:2 * d].astype(BF16)
    g2_ref[...] = out[:, 2 * d:].astype(BF16)


def _hy_in(x, g, sh, sc, w, b, sw, sb, tm):
    s, d = x.shape
    n3 = w.shape[1]
    r8 = tm // 8
    nblk8 = s // 8
    row = lambda i: (i, 0)
    out = jax.ShapeDtypeStruct((s, d), BF16)
    return pl.pallas_call(
        functools.partial(_hy_in_kernel, tm=tm, seq=s),
        grid=(s // tm,),
        in_specs=[
            pl.BlockSpec((8, d), lambda i: (jnp.maximum(i * r8 - 1, 0), 0)),
            pl.BlockSpec((tm, d), row),
            pl.BlockSpec((8, d), lambda i: (jnp.minimum((i + 1) * r8, nblk8 - 1), 0)),
            _const_spec((1, d)), _const_spec((1, d)), _const_spec((1, d)),
            _const_spec((d, n3)), _const_spec((1, n3)), _const_spec((3, n3)), _const_spec((1, n3)),
        ],
        out_specs=[pl.BlockSpec((tm, d), row)] * 3,
        out_shape=[out, out, out],
        compiler_params=_params("parallel"),
        name="hyena_in_proj",
    )(x, x, x, g, sh, sc, w, b, sw, sb)


def _hy_filter_kernel(z_ref, w1_ref, b1_ref, w2_ref, b2_ref, w3_ref, b3_ref, fr_ref, w4_ref, dl_ref,
                      hf_ref, nrm_ref):
    i = pl.program_id(0)
    hp = lax.Precision.HIGHEST
    z = z_ref[...]
    fr = fr_ref[...]
    a = jnp.sin(fr * (jnp.dot(z, w1_ref[...], preferred_element_type=F32, precision=hp) + b1_ref[...]))
    a = jnp.sin(fr * (jnp.dot(a, w2_ref[...], preferred_element_type=F32, precision=hp) + b2_ref[...]))
    a = jnp.sin(fr * (jnp.dot(a, w3_ref[...], preferred_element_type=F32, precision=hp) + b3_ref[...]))
    hf = jnp.dot(a.astype(BF16), w4_ref[...], preferred_element_type=F32)
    decay = jnp.exp(-z[:, 0:1] * jnp.abs(dl_ref[...]))
    d = decay.shape[1]
    cols = []
    for q in range(4):
        cols.append(hf[:, q * d:(q + 1) * d] * decay)
    hf = jnp.concatenate(cols, axis=1)

    @pl.when(i == 0)
    def _():
        nrm_ref[...] = jnp.zeros(nrm_ref.shape, F32)

    nrm_ref[...] += jnp.sum(jnp.abs(hf), axis=0, keepdims=True)
    t0 = (i == 0) & (lax.broadcasted_iota(jnp.int32, hf.shape, 0) == 0)
    col = lax.broadcasted_iota(jnp.int32, hf.shape, 1)
    bwd = ((col >= d) & (col < 2 * d)) | (col >= 3 * d)
    hf_ref[...] = jnp.where(t0 & bwd, 0.0, hf).astype(BF16)


def _hy_filter(z, w1, b1, w2, b2, w3, b3, fr, w4, deltas, tl):
    seq = z.shape[0]
    n4 = w4.shape[1]
    ff = w2.shape[0]
    return pl.pallas_call(
        _hy_filter_kernel,
        grid=(seq // tl,),
        in_specs=[pl.BlockSpec((tl, z.shape[1]), lambda i: (i, 0)),
                  _const_spec(w1.shape), _const_spec((1, ff)), _const_spec(w2.shape), _const_spec((1, ff)),
                  _const_spec(w3.shape), _const_spec((1, ff)), _const_spec((1, ff)), _const_spec(w4.shape),
                  _const_spec(deltas.shape)],
        out_specs=[pl.BlockSpec((tl, n4), lambda i: (i, 0)), _const_spec((1, n4))],
        out_shape=[jax.ShapeDtypeStruct((seq, n4), BF16), jax.ShapeDtypeStruct((1, n4), F32)],
        compiler_params=_params("arbitrary"),
        name="hyena_filter_mlp",
    )(z, w1, b1, w2, b2, w3, b3, fr, w4, deltas)


def _dft_a_kernel(f_ref, u_ref, o_ref):
    o_ref[...] = jnp.dot(f_ref[...], u_ref[...], preferred_element_type=F32).astype(o_ref.dtype)


def _dft_a(f1, u2, tn):
    m, k = f1.shape
    cols = u2.shape[1]
    return pl.pallas_call(
        _dft_a_kernel,
        grid=(cols // tn,),
        in_specs=[_const_spec((m, k)), pl.BlockSpec((k, tn), lambda j: (0, j))],
        out_specs=pl.BlockSpec((m, tn), lambda j: (0, j)),
        out_shape=jax.ShapeDtypeStruct((m, cols), BF16),
        compiler_params=_params("parallel"),
        name="dft_stage_a",
    )(f1, u2)


def _spectrum_kernel(wf_ref, a_ref, inv_ref, k_ref, *, kb, n2):
    inv = inv_ref[...]
    d = inv.shape[1]
    for b in range(kb):
        a = a_ref[:, b].reshape(2 * n2, 2 * d)
        x = jnp.dot(wf_ref[b], a, preferred_element_type=F32)
        xr, xi = x[:n2], x[n2:]
        k_ref[0, b] = (xr[:, :d] + xr[:, d:]) * inv
        k_ref[1, b] = (xi[:, :d] - xi[:, d:]) * inv


def _filter_spectrum(wf, a4, inv_norm, kb):
    _, kh, n2, c4 = a4.shape
    d = c4 // 4
    return pl.pallas_call(
        functools.partial(_spectrum_kernel, kb=kb, n2=n2),
        grid=(2, kh // kb),
        in_specs=[pl.BlockSpec((kb, 2 * n2, 2 * n2), lambda o, j: (j, 0, 0)),
                  pl.BlockSpec((2, kb, n2, 2 * d), lambda o, j: (0, j, 0, o)),
                  pl.BlockSpec((None, 1, d), lambda o, j: (o, 0, 0))],
        out_specs=pl.BlockSpec((None, 2, kb, n2, d), lambda o, j: (o, 0, j, 0, 0)),
        out_shape=jax.ShapeDtypeStruct((2, 2, kh, n2, d), F32),
        compiler_params=_params("parallel", "parallel"),
        name="filter_spectrum",
    )(wf, a4, inv_norm)


def _dft_mid_kernel(wf_ref, wi_ref, a_ref, k_ref, o_ref, *, kb, n2):
    c = a_ref.shape[-1]
    for b in range(kb):
        a = a_ref[:, b].reshape(2 * n2, c)
        x = jnp.dot(wf_ref[b], a, preferred_element_type=F32)
        xr, xi = x[:n2], x[n2:]
        kr, ki = k_ref[0, b], k_ref[1, b]
        y = jnp.concatenate([xr * kr - xi * ki, xr * ki + xi * kr], axis=0).astype(BF16)
        bb = jnp.dot(wi_ref[b], y, preferred_element_type=F32)
        o_ref[:, b] = bb.reshape(2, n2, c).astype(o_ref.dtype)


def _dft_mid(wf, wi, a4, kspec, order, kb):
    _, kh, n2, c = a4.shape
    return pl.pallas_call(
        functools.partial(_dft_mid_kernel, kb=kb, n2=n2),
        grid=(kh // kb,),
        in_specs=[pl.BlockSpec((kb, 2 * n2, 2 * n2), lambda j: (j, 0, 0)),
                  pl.BlockSpec((kb, 2 * n2, 2 * n2), lambda j: (j, 0, 0)),
                  pl.BlockSpec((2, kb, n2, c), lambda j: (0, j, 0, 0)),
                  pl.BlockSpec((None, 2, kb, n2, c), lambda j: (order, 0, j, 0, 0))],
        out_specs=pl.BlockSpec((2, kb, n2, c), lambda j: (0, j, 0, 0)),
        out_shape=jax.ShapeDtypeStruct(a4.shape, BF16),
        compiler_params=_params("parallel"),
        name="dft_middle",
    )(wf, wi, a4, kspec)


def _dft_c_kernel(f_ref, b_ref, u_ref, g_ref, bias_ref, o_ref):
    y = jnp.dot(f_ref[...], b_ref[...], preferred_element_type=F32)
    u = u_ref[...].astype(F32)
    o_ref[...] = (g_ref[...].astype(F32) * (y + u * bias_ref[...])).astype(o_ref.dtype)


def _dft_c(f3, b2, u2, g2, bias_t, tn):
    m, k = f3.shape
    cols = b2.shape[1]
    col = lambda j: (0, j)
    return pl.pallas_call(
        _dft_c_kernel,
        grid=(cols // tn,),
        in_specs=[_const_spec((m, k)), pl.BlockSpec((k, tn), col), pl.BlockSpec((m, tn), col),
                  pl.BlockSpec((m, tn), col), _const_spec((1, tn))],
        out_specs=pl.BlockSpec((m, tn), col),
        out_shape=jax.ShapeDtypeStruct((m, cols), BF16),
        compiler_params=_params("parallel"),
        name="dft_stage_c",
    )(f3, b2, u2, g2, bias_t)


def _dft_constants(seq):
    n = 2 * seq
    n2 = int(round(math.sqrt(seq)))
    n1 = n // n2
    assert n1 * n2 == n and n1 == 2 * n2
    kreal = n1 // 2 + 1
    kh = ((kreal + 7) // 8) * 8
    k1 = jnp.arange(kh, dtype=jnp.int32)
    live = (k1 < kreal)
    nn1 = jnp.arange(n1 // 2, dtype=jnp.int32)
    ph = ((k1[:, None] * nn1[None, :]) % n1).astype(F32) * (2.0 * math.pi / n1)
    lv = live[:, None].astype(F32)
    f1 = jnp.concatenate([jnp.cos(ph) * lv, -jnp.sin(ph) * lv], axis=0).astype(BF16)
    kk2 = jnp.arange(n2, dtype=jnp.int32)
    nn2 = jnp.arange(n2, dtype=jnp.int32)
    num = (k1[:, None, None] * nn2[None, None, :] + n1 * kk2[None, :, None] * nn2[None, None, :]) % n
    ang = num.astype(F32) * (2.0 * math.pi / n)
    mr, mi = jnp.cos(ang), -jnp.sin(ang)
    wf = jnp.concatenate([jnp.concatenate([mr, -mi], axis=2),
                          jnp.concatenate([mi, mr], axis=2)], axis=1).astype(BF16)
    gr, gi = jnp.swapaxes(mr, 1, 2), -jnp.swapaxes(mi, 1, 2)
    wi = jnp.concatenate([jnp.concatenate([gr, -gi], axis=2),
                          jnp.concatenate([gi, gr], axis=2)], axis=1).astype(BF16)
    ck = jnp.where((k1 == 0) | (k1 == n1 // 2), 1.0, 2.0) * live.astype(F32) / n
    ph3 = ((nn1[:, None] * k1[None, :]) % n1).astype(F32) * (2.0 * math.pi / n1)
    f3 = jnp.concatenate([jnp.cos(ph3) * ck[None, :], -jnp.sin(ph3) * ck[None, :]], axis=1).astype(BF16)
    return n1, n2, kh, f1, wf, wi, f3


def _hyena_filter_inputs(seq):
    t = jnp.linspace(0.0, 1.0, seq, dtype=F32)[:, None]
    w = 2.0 * math.pi * jnp.arange(seq, dtype=F32)[:, None] / seq
    f = jnp.linspace(1e-4, HY_BANDS - 1, HY_BANDS, dtype=F32)[None, :]
    z = jnp.concatenate([t, jnp.cos(f * w), -jnp.sin(f * w)], axis=-1)
    return jnp.pad(z, ((0, 0), (0, HY_FILTER_FF - z.shape[1])))


def _hyena_mixer(x, g, sh, sc, gate, w_in, b_in, short_w, short_b, fw1, fb1, fw2, fb2, fw3, fb3,
                 sin_freq, fw4, filter_bias, w_out, b_out):
    seq, d = x.shape
    n1, n2, kh, f1, wf, wi, f3 = _dft_constants(seq)
    kb = 8
    v, g1, g2 = _hy_in(x, g, sh, sc, w_in.astype(BF16), b_in.reshape(1, -1), short_w, short_b.reshape(1, -1),
                       tm=256)

    z = _hyena_filter_inputs(seq)
    fw1p = jnp.pad(fw1, ((0, HY_FILTER_FF - fw1.shape[0]), (0, 0)))
    deltas = jnp.linspace(math.log(HY_DECAY_TARGET) / HY_SLOW_DECAY,
                          math.log(HY_DECAY_TARGET) / HY_FAST_DECAY, d, dtype=F32)[None, :]
    hf, nrm = _hy_filter(z, fw1p, fb1.reshape(1, -1), fw2, fb2.reshape(1, -1), fw3, fb3.reshape(1, -1),
                         sin_freq.reshape(1, -1), fw4.astype(BF16), deltas, tl=512)
    nrm = nrm.reshape(2, 2, d)
    inv_norm = (1.0 / (nrm[:, 0] + nrm[:, 1])).reshape(2, 1, d)
    half = n1 // 2
    fa = _dft_a(f1, hf.reshape(half, n2 * 4 * d), tn=4096)
    kspec = _filter_spectrum(wf, fa.reshape(2, kh, n2, 4 * d), inv_norm, kb)

    def conv(u, gt, order):
        a = _dft_a(f1, u.reshape(half, n2 * d), tn=4096)
        b = _dft_mid(wf, wi, a.reshape(2, kh, n2, d), kspec, order, kb)
        tn = 2 * d
        bias_t = jnp.tile(filter_bias[order].reshape(1, d), (1, tn // d))
        zz = _dft_c(f3, b.reshape(2 * kh, n2 * d), u.reshape(half, n2 * d), gt.reshape(half, n2 * d), bias_t, tn)
        return zz.reshape(seq, d)

    z1 = conv(v, g1, 0)
    z2 = conv(z1, g2, 1)
    return _resid_proj1(z2, w_out.astype(BF16), b_out.reshape(1, -1), x, gate, tm=512)


def _attention_mixer(x, ctx, g, sh, sc, sh_c, sc_c, gate, w_in, q_norm_g, kv_norm_g, w_uq, w_ukv, sink, w_out):
    s, d = x.shape
    n_ctx = ctx.shape[0]
    w_aug, wq, wkv = _attn_weights(w_in, w_uq, w_ukv)
    cos_m, sin_m = _rope_tables(s, n_ctx, MLA_ROPE)
    tot = s + n_ctx
    cm = jnp.concatenate([jnp.ones((tot, MLA_NOPE), F32), cos_m, jnp.zeros((tot, LANES - 96), F32)], axis=1)
    sm = jnp.concatenate([jnp.zeros((tot, MLA_NOPE), F32), sin_m, jnp.zeros((tot, LANES - 96), F32)], axis=1)
    cos_g, sin_g = _rope_tables(s, n_ctx, GQA_HEAD_DIM)
    cg = jnp.concatenate([cos_g, cos_g], axis=1)
    sg = jnp.concatenate([sin_g, sin_g], axis=1)
    sh2 = jnp.stack([sh_c, sh], axis=0)
    sc2 = jnp.stack([sc_c, sc], axis=0)
    mq, mk, mv, gq, gk, gv = _attn_prep(x, ctx, g, sh2, sc2, w_aug, q_norm_g.reshape(1, -1),
                                        kv_norm_g.reshape(1, -1), wq, wkv, cm, sm, cg, sg)
    tk = next(t for t in (3328, 1280, n_ctx) if tot % t == 0)
    mla_out = _mla_attention(mq, mk, mv, tq=min(1024, s), tk=tk)
    gqa_out = _gqa_attention(sink, gq, gk, gv, n_ctx)
    wo = w_out.astype(BF16)
    nm = MLA_HEADS * MLA_V
    return _resid_proj2(mla_out, gqa_out, wo[:nm], wo[nm:], x, gate, tm=512)


def kernel(x, c, ctx, c_ctx, mod_w, mod_b, norm1_g, norm2_g, attn_w_in, mla_q_norm_g, mla_kv_norm_g,
           mla_w_uq, mla_w_ukv, gqa_sink, attn_w_out, hy_w_in, hy_b_in, hy_short_w, hy_short_b,
           hy_ffn_w1, hy_ffn_b1, hy_ffn_w2, hy_ffn_b2, hy_ffn_w3, hy_ffn_b3, hy_sin_freq, hy_ffn_w4,
           hy_filter_bias, hy_w_out, hy_b_out, moe_router_w, moe_router_b, moe_w1, moe_b1, moe_w2,
           moe_b2, final_norm_g):
    batch, seq, d = x.shape
    assert batch == 1
    depth = mod_w.shape[0]
    xs = x.reshape(seq, d)
    cs = ctx.reshape(-1, d)
    mod = _modulation(c, c_ctx, mod_w, mod_b)
    fg = final_norm_g.reshape(1, d)
    for i in range(depth):
        m = mod[i, 0].reshape(6, 1, d)
        mc = mod[i, 1].reshape(6, 1, d)
        g1 = norm1_g[i].reshape(1, d)
        j = i // 2
        if i % 2 == 0:
            xs = _attention_mixer(xs, cs, g1, m[0], m[1], mc[0], mc[1], m[2], attn_w_in[j], mla_q_norm_g[j],
                                  mla_kv_norm_g[j], mla_w_uq[j], mla_w_ukv[j], gqa_sink[j], attn_w_out[j])
        else:
            xs = _hyena_mixer(xs, g1, m[0], m[1], m[2], hy_w_in[j], hy_b_in[j], hy_short_w[j], hy_short_b[j],
                              hy_ffn_w1[j], hy_ffn_b1[j], hy_ffn_w2[j], hy_ffn_b2[j], hy_ffn_w3[j], hy_ffn_b3[j],
                              hy_sin_freq[j], hy_ffn_w4[j], hy_filter_bias[j], hy_w_out[j], hy_b_out[j])
        xs = _moe_layer(xs, norm2_g[i].reshape(1, d), m[3], m[4], m[5], moe_router_w[i], moe_router_b[i],
                        moe_w1, moe_b1, moe_w2, moe_b2, fg, final_norm=(i == depth - 1), layer=i)
    return xs.reshape(batch, seq, d)
```
